```python
import math
import jax, jax.numpy as jnp
from jax import lax
import numpy as np

D_MODEL = 1024
BATCH = 8
SEQ = 4096
DEPTH = 2

GRID_W = 64
CTX_LEN = 256
RET_HEADS = 4
RET_DK = 64
RET_DV = 64
RET_CHUNK = 128
MLA_HEADS = 8
MLA_Q_RANK = 256
MLA_KV_RANK = 128
MLA_NOPE = 64
MLA_ROPE = 32
MLA_DV = 64
MLA_QBLOCK = 128
WIN_Q_HEADS = 4
WIN_KV_HEADS = 2
WIN_DH = 64
WINDOW = 128
WIN_BLOCK = 128
N_EXPERTS = 16
EXPERT_FF = 768
CAPACITY_FACTOR = 2

ROPE_BASE = 10000.0
NORM_EPS = 1e-6
GN_EPS = 1e-5
NEG_INF = -1e30

IN_SPLITS = (RET_HEADS * RET_DK, RET_HEADS * RET_DK, RET_HEADS * RET_DV, RET_HEADS * RET_DV,
             MLA_Q_RANK, MLA_KV_RANK, MLA_ROPE,
             WIN_Q_HEADS * WIN_DH, WIN_KV_HEADS * WIN_DH, WIN_KV_HEADS * WIN_DH)
IN_COLS = sum(IN_SPLITS)

kernel_name = 'hybrid_ret_mla_swa_ec_diffusion'


def rmsnorm(x, g):
    xf = x.astype(jnp.float32)
    y = xf * lax.rsqrt(jnp.mean(xf * xf, axis=-1, keepdims=True) + NORM_EPS)
    return (y * g.astype(jnp.float32)).astype(x.dtype)


def modulate(h, shift, scale):
    return h * (1 + scale) + shift


def split_cols(p):
    offs = [int(o) for o in np.cumsum(IN_SPLITS)[:-1]]
    return jnp.split(p, offs, axis=-1)


def rope_1d(x, pos):
    dh = x.shape[-1]
    inv = ROPE_BASE ** (-jnp.arange(0, dh, 2, dtype=jnp.float32) / dh)
    ang = pos.astype(jnp.float32)[:, None] * inv[None, :]
    cos = jnp.cos(ang)[:, None, :].astype(x.dtype)
    sin = jnp.sin(ang)[:, None, :].astype(x.dtype)
    x1, x2 = jnp.split(x, 2, axis=-1)
    return jnp.concatenate([x1 * cos - x2 * sin, x1 * sin + x2 * cos], axis=-1)


def axial_rope(x, row, col):
    xr, xc = jnp.split(x, 2, axis=-1)
    return jnp.concatenate([rope_1d(xr, row), rope_1d(xc, col)], axis=-1)


def retention_scan(q, k, v, log_g, s0, inclusive):
    b, h, l, dk = q.shape
    dv = v.shape[-1]
    C = RET_CHUNK
    n = l // C
    qc = q.reshape(b, h, n, C, dk)
    kc = k.reshape(b, h, n, C, dk)
    vc = v.reshape(b, h, n, C, dv)
    idx = jnp.arange(C, dtype=jnp.float32)
    diff = idx[:, None] - idx[None, :]
    keep = (diff >= 0) if inclusive else (diff > 0)
    lg = log_g.astype(jnp.float32)
    dmat = jnp.where(keep, jnp.exp(lg[:, None, None] * jnp.where(keep, diff, 0.0)), 0.0).astype(q.dtype)
    scores = jnp.einsum('bhncd,bhnmd->bhncm', qc, kc) * dmat[:, None]
    y = jnp.einsum('bhncm,bhnme->bhnce', scores, vc)
    k_w = jnp.exp(lg[:, None] * (C - 1 - idx)[None, :]).astype(q.dtype)
    q_w = jnp.exp(lg[:, None] * (idx + 1)[None, :]).astype(q.dtype)
    chunk_kv = jnp.einsum('bhncd,bhnce->nbhde', kc * k_w[:, None, :, None], vc)
    g_chunk = jnp.exp(lg * C).astype(q.dtype)[:, None, None]

    def step(s, kv_n):
        return g_chunk * s + kv_n, s

    s_final, s_prev = lax.scan(step, s0.astype(q.dtype), chunk_kv)
    y = y + jnp.einsum('bhncd,nbhde->bhnce', qc * q_w[:, None, :, None], s_prev)
    return y.reshape(b, h, l, dv), s_final


def head_groupnorm(y):
    yf = y.astype(jnp.float32)
    mu = jnp.mean(yf, axis=-1, keepdims=True)
    var = jnp.mean(jnp.square(yf - mu), axis=-1, keepdims=True)
    return ((yf - mu) * lax.rsqrt(var + GN_EPS)).astype(y.dtype)


def retention_group(q_x, k_x, v_x, g_x, q_c, k_c, v_c, g_c, decay_f, decay_b, need_ctx):
    def heads(t):
        bb, ll, _ = t.shape
        return t.reshape(bb, ll, RET_HEADS, -1).transpose(0, 2, 1, 3)

    def flip(t):
        return t[:, :, ::-1]

    lg_f = jnp.log1p(-jnp.exp2(decay_f.astype(jnp.float32)))
    lg_b = jnp.log1p(-jnp.exp2(decay_b.astype(jnp.float32)))
    scale = RET_DK ** -0.5
    qx, kx, vx = heads(q_x), heads(k_x) * scale, heads(v_x)
    qc, kc, vc = heads(q_c), heads(k_c) * scale, heads(v_c)
    zero = jnp.zeros((qx.shape[0], RET_HEADS, RET_DK, RET_DV), qx.dtype)
    yc_f, s_f = retention_scan(qc, kc, vc, lg_f, zero, True)
    yc_b, s_b = retention_scan(flip(qc), flip(kc), flip(vc), lg_b, zero, False)
    yx_f, _ = retention_scan(qx, kx, vx, lg_f, s_f, True)
    yx_b, _ = retention_scan(flip(qx), flip(kx), flip(vx), lg_b, s_b, False)

    def finish(y, g):
        bb, hh, ll, dd = y.shape
        y = head_groupnorm(y).transpose(0, 2, 1, 3).reshape(bb, ll, hh * dd)
        return y * jax.nn.silu(g)

    out_x = finish(yx_f + flip(yx_b), g_x)
    out_c = finish(yc_f + flip(yc_b), g_c) if need_ctx else None
    return out_x, out_c


def mla_group(cq_x, ckv_x, kr_x, cq_c, ckv_c, kr_c, row, col, qnorm_g, kvnorm_g, w_uq, w_uk, w_uv, need_ctx):
    scale = (MLA_NOPE + MLA_ROPE) ** -0.5

    def queries(cq):
        q = jnp.einsum('blr,rhe->blhe', rmsnorm(cq, qnorm_g), w_uq)
        return q[..., :MLA_NOPE], q[..., MLA_NOPE:]

    def attend(ql, qr, kv, kr):
        s = (jnp.einsum('bhqr,bkr->bhqk', ql, kv) + jnp.einsum('bhqp,bkp->bhqk', qr, kr)).astype(jnp.float32) * scale
        p = jax.nn.softmax(s, axis=-1).astype(kv.dtype)
        return jnp.einsum('bhqk,bkr->bhqr', p, kv)

    def up(ol):
        o = jnp.einsum('bhlr,rhd->blhd', ol, w_uv)
        return o.reshape(o.shape[0], o.shape[1], -1)

    kv_c = rmsnorm(ckv_c, kvnorm_g)
    kv_x = rmsnorm(ckv_x, kvnorm_g)
    kr_xr = axial_rope(kr_x[:, :, None, :], row, col)[:, :, 0, :]
    kv_all = jnp.concatenate([kv_c, kv_x], axis=1)
    kr_all = jnp.concatenate([kr_c, kr_xr], axis=1)
    qn_x, qr_x = queries(cq_x)
    qr_x = axial_rope(qr_x, row, col).transpose(0, 2, 1, 3)
    ql_x = jnp.einsum('blhd,rhd->bhlr', qn_x, w_uk)
    b, h, l, r = ql_x.shape
    nb = l // MLA_QBLOCK

    def blocks(t):
        return t.reshape(b, h, nb, MLA_QBLOCK, t.shape[-1]).transpose(2, 0, 1, 3, 4)

    ol = lax.map(lambda qs: attend(qs[0], qs[1], kv_all, kr_all), (blocks(ql_x), blocks(qr_x)))
    ol = ol.transpose(1, 2, 0, 3, 4).reshape(b, h, l, r)
    out_x = up(ol)
    out_c = None
    if need_ctx:
        qn_c, qr_c = queries(cq_c)
        ql_c = jnp.einsum('blhd,rhd->bhlr', qn_c, w_uk)
        out_c = up(attend(ql_c, qr_c.transpose(0, 2, 1, 3), kv_c, kr_c))
    return out_x, out_c


def window_group(q_x, k_x, v_x, q_c, k_c, v_c, row, col, sink, need_ctx):
    b, l, _ = q_x.shape
    lc = q_c.shape[1]
    hk, g, d = WIN_KV_HEADS, WIN_Q_HEADS // WIN_KV_HEADS, WIN_DH
    W = WIN_BLOCK
    nb = l // W
    scale = d ** -0.5
    qx = axial_rope(q_x.reshape(b, l, WIN_Q_HEADS, d), row, col).reshape(b, l, hk, g, d)
    kx = axial_rope(k_x.reshape(b, l, hk, d), row, col)
    vx = v_x.reshape(b, l, hk, d)
    kc = k_c.reshape(b, lc, hk, d)
    vc = v_c.reshape(b, lc, hk, d)
    sink_l = sink.astype(jnp.float32).reshape(hk, g, 1, 1)

    def band(t):
        tp = jnp.pad(t, ((0, 0), (W, W), (0, 0), (0, 0))).reshape(b, nb + 2, W, hk, d)
        return jnp.concatenate([tp[:, :-2], tp[:, 1:-1], tp[:, 2:]], axis=2)

    kb, vb = band(kx), band(vx)
    qi = jnp.arange(nb)[:, None, None] * W + jnp.arange(W)[None, :, None]
    kj = (jnp.arange(nb)[:, None, None] - 1) * W + jnp.arange(3 * W)[None, None, :]
    mask = (jnp.abs(kj - qi) <= WINDOW) & (kj >= 0) & (kj < l)
    qb = qx.reshape(b, nb, W, hk, g, d)
    s_loc = jnp.einsum('bnqhgd,bnkhd->bnhgqk', qb, kb).astype(jnp.float32) * scale
    s_loc = jnp.where(mask[None, :, None, None], s_loc, NEG_INF)
    s_ctx = jnp.einsum('bnqhgd,bkhd->bnhgqk', qb, kc).astype(jnp.float32) * scale
    s_sink = jnp.broadcast_to(sink_l, s_loc.shape[:-1] + (1,))
    p = jax.nn.softmax(jnp.concatenate([s_loc, s_ctx, s_sink], axis=-1), axis=-1).astype(vx.dtype)
    o = (jnp.einsum('bnhgqk,bnkhd->bnqhgd', p[..., :3 * W], vb)
         + jnp.einsum('bnhgqk,bkhd->bnqhgd', p[..., 3 * W:3 * W + lc], vc))
    out_x = o.reshape(b, l, WIN_Q_HEADS * d)
    out_c = None
    if need_ctx:
        s = jnp.einsum('bqhgd,bkhd->bhgqk', q_c.reshape(b, lc, hk, g, d), kc).astype(jnp.float32) * scale
        s = jnp.concatenate([s, jnp.broadcast_to(sink_l, s.shape[:-1] + (1,))], axis=-1)
        pc = jax.nn.softmax(s, axis=-1).astype(vc.dtype)
        out_c = jnp.einsum('bhgqk,bkhd->bqhgd', pc[..., :lc], vc).reshape(b, lc, WIN_Q_HEADS * d)
    return out_x, out_c


def hybrid_mixer(px, pc, row, col, ret_decay_f, ret_decay_b, mla_qnorm_g, mla_kvnorm_g,
                 mla_w_uq, mla_w_uk, mla_w_uv, win_sink, need_ctx):
    (rq_x, rk_x, rv_x, rg_x, cq_x, ckv_x, kr_x, wq_x, wk_x, wv_x) = split_cols(px)
    (rq_c, rk_c, rv_c, rg_c, cq_c, ckv_c, kr_c, wq_c, wk_c, wv_c) = split_cols(pc)
    ret_x, ret_c = retention_group(rq_x, rk_x, rv_x, rg_x, rq_c, rk_c, rv_c, rg_c,
                                   ret_decay_f, ret_decay_b, need_ctx)
    mla_x, mla_c = mla_group(cq_x, ckv_x, kr_x, cq_c, ckv_c, kr_c, row, col, mla_qnorm_g, mla_kvnorm_g,
                             mla_w_uq, mla_w_uk, mla_w_uv, need_ctx)
    win_x, win_c = window_group(wq_x, wk_x, wv_x, wq_c, wk_c, wv_c, row, col, win_sink, need_ctx)
    y_x = jnp.concatenate([ret_x, mla_x, win_x], axis=-1)
    y_c = jnp.concatenate([ret_c, mla_c, win_c], axis=-1) if need_ctx else None
    return y_x, y_c


def expert_choice_ffn(h, router_w, w_gate, w_up, w_down):
    b, l, d = h.shape
    cap = CAPACITY_FACTOR * l // N_EXPERTS
    aff = jax.nn.softmax(jnp.einsum('bld,de->ble', h, router_w).astype(jnp.float32), axis=-1)
    gate, idx = lax.top_k(aff.transpose(0, 2, 1), cap)
    xs = jax.vmap(lambda hb, ib: hb[ib])(h, idx)
    hid = jax.nn.silu(jnp.einsum('becd,edf->becf', xs, w_gate)) * jnp.einsum('becd,edf->becf', xs, w_up)
    ys = jnp.einsum('becf,efd->becd', hid, w_down) * gate[..., None].astype(h.dtype)
    return jax.vmap(lambda yb, ib: jnp.zeros((l, d), h.dtype).at[ib.reshape(-1)].add(yb.reshape(-1, d)))(ys, idx)


def setup_inputs(seed: int = 0) -> dict:
    key = jax.random.key(seed)
    ks = iter(jax.random.split(key, 32))

    def nrm(shape, scale):
        return jax.random.normal(next(ks), shape, jnp.float32) * scale

    L, D = DEPTH, D_MODEL
    sched = -5.0 - jnp.arange(RET_HEADS, dtype=jnp.float32)
    return {
        'x': nrm((BATCH, SEQ, D), 1.0),
        'c': nrm((BATCH, D), 1.0),
        'ctx': nrm((BATCH, CTX_LEN, D), 1.0),
        'c_ctx': nrm((D,), 1.0),
        'norm1_g': 1.0 + nrm((L, D), 0.05),
        'norm2_g': 1.0 + nrm((L, D), 0.05),
        'ada_w': nrm((L, D, 6 * D), 0.5 * D ** -0.5),
        'ada_b': nrm((L, 6 * D), 0.02),
        'w_in': nrm((L, D, IN_COLS), D ** -0.5),
        'ret_decay_f': sched + nrm((L, RET_HEADS), 0.1),
        'ret_decay_b': sched + nrm((L, RET_HEADS), 0.1),
        'mla_qnorm_g': 1.0 + nrm((L, MLA_Q_RANK), 0.05),
        'mla_kvnorm_g': 1.0 + nrm((L, MLA_KV_RANK), 0.05),
        'mla_w_uq': nrm((L, MLA_Q_RANK, MLA_HEADS, MLA_NOPE + MLA_ROPE), MLA_Q_RANK ** -0.5),
        'mla_w_uk': nrm((L, MLA_KV_RANK, MLA_HEADS, MLA_NOPE), MLA_KV_RANK ** -0.5),
        'mla_w_uv': nrm((L, MLA_KV_RANK, MLA_HEADS, MLA_DV), MLA_KV_RANK ** -0.5),
        'win_sink': nrm((L, WIN_Q_HEADS), 0.5),
        'w_out': nrm((L, D, D), D ** -0.5),
        'router_w': nrm((L, D, N_EXPERTS), D ** -0.5),
        'exp_w_gate': nrm((L, N_EXPERTS, D, EXPERT_FF), D ** -0.5),
        'exp_w_up': nrm((L, N_EXPERTS, D, EXPERT_FF), D ** -0.5),
        'exp_w_down': nrm((L, N_EXPERTS, EXPERT_FF, D), EXPERT_FF ** -0.5),
        'final_g': 1.0 + nrm((D,), 0.05),
    }


def reference(x, c, ctx, c_ctx, norm1_g, norm2_g, ada_w, ada_b, w_in, ret_decay_f, ret_decay_b,
              mla_qnorm_g, mla_kvnorm_g, mla_w_uq, mla_w_uk, mla_w_uv, win_sink, w_out,
              router_w, exp_w_gate, exp_w_up, exp_w_down, final_g):
    n_tok = x.shape[1]
    rows = n_tok // GRID_W
    row = jnp.repeat(jnp.arange(rows, dtype=jnp.int32), GRID_W)
    col = jnp.tile(jnp.arange(GRID_W, dtype=jnp.int32), rows)
    for layer in range(DEPTH):
        need_ctx = layer < DEPTH - 1
        mod_x = [m[:, None, :] for m in jnp.split(jax.nn.silu(c) @ ada_w[layer] + ada_b[layer], 6, axis=-1)]
        mod_c = jnp.split(jax.nn.silu(c_ctx) @ ada_w[layer] + ada_b[layer], 6, axis=-1)
        hx = modulate(rmsnorm(x, norm1_g[layer]), mod_x[0], mod_x[1])
        hc = modulate(rmsnorm(ctx, norm1_g[layer]), mod_c[0], mod_c[1])
        yx, yc = hybrid_mixer(hx @ w_in[layer], hc @ w_in[layer], row, col,
                              ret_decay_f[layer], ret_decay_b[layer], mla_qnorm_g[layer], mla_kvnorm_g[layer],
                              mla_w_uq[layer], mla_w_uk[layer], mla_w_uv[layer], win_sink[layer], need_ctx)
        x = x + mod_x[2] * (yx @ w_out[layer])
        hx = modulate(rmsnorm(x, norm2_g[layer]), mod_x[3], mod_x[4])
        x = x + mod_x[5] * expert_choice_ffn(hx, router_w[layer], exp_w_gate[layer], exp_w_up[layer], exp_w_down[layer])
        if need_ctx:
            ctx = ctx + mod_c[2] * (yc @ w_out[layer])
            hc = modulate(rmsnorm(ctx, norm2_g[layer]), mod_c[3], mod_c[4])
            ctx = ctx + mod_c[5] * expert_choice_ffn(hc, router_w[layer], exp_w_gate[layer], exp_w_up[layer], exp_w_down[layer])
    return rmsnorm(x, final_g)
```

```python
import functools

import numpy as np
import jax
import jax.numpy as jnp
from jax import lax
from jax.experimental import pallas as pl
from jax.experimental.pallas import tpu as pltpu

F32 = jnp.float32
BF16 = jnp.bfloat16

D_MODEL = 1024
DEPTH = 2
GRID_W = 64
RET_HEADS = 4
RET_DK = 64
RET_CHUNK = 128
MLA_HEADS = 8
MLA_Q_RANK = 256
MLA_KV_RANK = 128
MLA_NOPE = 64
MLA_ROPE = 32
MLA_DV = 64
WIN_Q_HEADS = 4
WIN_KV_HEADS = 2
WIN_DH = 64
WINDOW = 128
N_EXPERTS = 16
EXPERT_FF = 768
CAPACITY_FACTOR = 2
ROPE_BASE = 10000.0
NORM_EPS = 1e-6
GN_EPS = 1e-5
NEG_INF = -1e30
LOG2E = 1.4426950408889634

LANES = 128
MOD_ROWS = 16
KCAT = MLA_KV_RANK + MLA_ROPE
VMEM_LIMIT = 56 * 1024 * 1024

C_RET = 0
C_CQ = 1024
C_CKV = 1280
C_WQ = 1408
C_WQP = 1664
C_WK = 1920
C_WKP = 2048
C_WV = 2176
C_KR = 2304
N_EXT = 2432


def _cparams(sem):
    return pltpu.CompilerParams(dimension_semantics=sem, vmem_limit_bytes=VMEM_LIMIT)


def _dot(a, b):
    return jnp.dot(a, b, preferred_element_type=F32)


def _dot_nt(a, b):
    return lax.dot_general(a, b, (((1,), (1,)), ((), ())), preferred_element_type=F32)


def _dot_tn(a, b):
    return lax.dot_general(a, b, (((0,), (0,)), ((), ())), preferred_element_type=F32)


def _rms(x, g):
    return x * lax.rsqrt(jnp.mean(x * x, axis=-1, keepdims=True) + NORM_EPS) * g


def _silu(x):
    return x * jax.nn.sigmoid(x)


def _mod_kernel(c_ref, w_ref, b_ref, o_ref):
    o_ref[0] = jnp.dot(_silu(c_ref[...]), w_ref[0], preferred_element_type=F32,
                       precision=lax.Precision.HIGHEST) + b_ref[0]


def _modulation(c_all, ada_w, ada_b):
    depth, d, n = ada_w.shape
    tn = 1024
    return pl.pallas_call(
        _mod_kernel,
        out_shape=jax.ShapeDtypeStruct((depth, MOD_ROWS, n), F32),
        grid=(depth, n // tn),
        in_specs=[pl.BlockSpec((MOD_ROWS, d), lambda l, j: (0, 0)),
                  pl.BlockSpec((1, d, tn), lambda l, j: (l, 0, j)),
                  pl.BlockSpec((1, 1, tn), lambda l, j: (l, 0, j))],
        out_specs=pl.BlockSpec((1, MOD_ROWS, tn), lambda l, j: (l, 0, j)),
        compiler_params=_cparams(("arbitrary", "arbitrary")),
        name="adaln_mod",
    )(c_all, ada_w, ada_b.reshape(depth, 1, n))


def _wabs_kernel(uq_ref, uk_ref, o_ref):
    o_ref[0] = jnp.dot(uq_ref[0, 0], uk_ref[0, 0], preferred_element_type=F32,
                       precision=lax.Precision.HIGHEST)


def _absorbed_q_weights(w_uq, w_uk):
    depth = w_uq.shape[0]
    uq = jnp.transpose(w_uq[..., :MLA_NOPE], (0, 2, 1, 3))
    uk = jnp.transpose(w_uk, (0, 2, 3, 1))
    return pl.pallas_call(
        _wabs_kernel,
        out_shape=jax.ShapeDtypeStruct((depth, MLA_Q_RANK, MLA_HEADS * MLA_KV_RANK), F32),
        grid=(depth, MLA_HEADS),
        in_specs=[pl.BlockSpec((1, 1, MLA_Q_RANK, MLA_NOPE), lambda l, h: (l, h, 0, 0)),
                  pl.BlockSpec((1, 1, MLA_NOPE, MLA_KV_RANK), lambda l, h: (l, h, 0, 0))],
        out_specs=pl.BlockSpec((1, MLA_Q_RANK, MLA_KV_RANK), lambda l, h: (l, 0, h)),
        compiler_params=_cparams(("arbitrary", "arbitrary")),
        name="mla_absorb",
    )(uq, uk)


def _inproj_kernel(*refs, rope, resid):
    it = iter(refs)
    x_ref = next(it)
    if resid:
        acc_ref, g5_ref = next(it), next(it)
    g1_ref, sh_ref, sc_ref, w_ref, qng_ref, kvg_ref = (next(it) for _ in range(6))
    if rope:
        cw_ref, sw_ref, cm_ref, sm_ref = (next(it) for _ in range(4))
    if resid:
        xo_ref = next(it)
    ret_ref, cqn_ref, kcat_ref, win_ref = (next(it) for _ in range(4))

    x = x_ref[0]
    if resid:
        x = x + g5_ref[0] * acc_ref[0]
        xo_ref[0] = x
    h = _rms(x, g1_ref[...]) * (1.0 + sc_ref[0]) + sh_ref[0]
    p = _dot(h.astype(BF16), w_ref[...])

    ret_ref[0] = p[:, C_RET:C_CQ]
    cqn_ref[0] = _rms(p[:, C_CQ:C_CKV], qng_ref[...]).astype(BF16)
    kvn = _rms(p[:, C_CKV:C_WQ], kvg_ref[...])
    wq, wk, wv = p[:, C_WQ:C_WQP], p[:, C_WK:C_WKP], p[:, C_WV:C_KR]
    kr = p[:, C_KR:C_KR + MLA_ROPE]
    if rope:
        wqp, wkp = p[:, C_WQP:C_WK], p[:, C_WKP:C_WV]
        krp = p[:, C_KR + MLA_ROPE:C_KR + 2 * MLA_ROPE]
        cw, sw = cw_ref[...], sw_ref[...]
        wq = jnp.concatenate([wq[:, :LANES] * cw + wqp[:, :LANES] * sw,
                              wq[:, LANES:] * cw + wqp[:, LANES:] * sw], axis=1)
        wk = wk * cw + wkp * sw
        kr = kr * cm_ref[...] + krp * sm_ref[...]
    kcat_ref[0, :, 0:MLA_KV_RANK] = kvn.astype(BF16)
    kcat_ref[0, :, MLA_KV_RANK:KCAT] = kr.astype(BF16)
    win_ref[0, :, 0:256] = wq.astype(BF16)
    win_ref[0, :, 256:384] = wk.astype(BF16)
    win_ref[0, :, 384:512] = wv.astype(BF16)


def _inproj(x, mod, mrow, layer, g1, w_ext, qng, kvg, tables, resid, tm):
    b, l, d = x.shape
    rope = tables is not None

    def mspec(k, lay):
        return pl.BlockSpec((1, 1, d), lambda bi, i: ((lay * MOD_ROWS + mrow(bi)) * 6 + k, 0, 0))

    tok = lambda w: pl.BlockSpec((1, tm, w), lambda bi, i: (bi, i, 0))
    const = lambda a: pl.BlockSpec(a.shape, lambda bi, i: (0,) * a.ndim)
    args, specs = [x], [tok(d)]
    if resid is not None:
        args += [resid[0], mod]
        specs += [tok(d), mspec(5, resid[1])]
    args += [g1, mod, mod, w_ext, qng, kvg]
    specs += [const(g1), mspec(0, layer), mspec(1, layer), const(w_ext), const(qng), const(kvg)]
    if rope:
        for t in tables:
            args.append(t)
            specs.append(pl.BlockSpec((tm, t.shape[1]), lambda bi, i: (i, 0)))
    out_shape, out_specs = [], []
    if resid is not None:
        out_shape.append(jax.ShapeDtypeStruct((b, l, d), F32))
        out_specs.append(tok(d))
    out_shape += [jax.ShapeDtypeStruct((b, l, 1024), F32), jax.ShapeDtypeStruct((b, l, MLA_Q_RANK), BF16),
                  jax.ShapeDtypeStruct((b, l, KCAT), BF16), jax.ShapeDtypeStruct((b, l, 512), BF16)]
    out_specs += [tok(1024), tok(MLA_Q_RANK), tok(KCAT), tok(512)]
    outs = pl.pallas_call(
        functools.partial(_inproj_kernel, rope=rope, resid=resid is not None),
        out_shape=out_shape, grid=(b, l // tm), in_specs=specs, out_specs=out_specs,
        compiler_params=_cparams(("arbitrary", "arbitrary")),
        name="inproj",
    )(*args)
    if resid is None:
        return (x,) + tuple(outs)
    return tuple(outs)


def _ret_kernel(dec_ref, cf_ref, cb_ref, xf_ref, xb_ref, ycf_ref, ycb_ref, yxf_ref, yxb_ref,
                s_ref, dm_ref, qw_ref, kw_ref, gc_ref, *, n_ctx):
    n = pl.program_id(1)
    c = RET_CHUNK
    dk = RET_DK

    @pl.when(n == 0)
    def _():
        s_ref[...] = jnp.zeros_like(s_ref)
        lg = jnp.log1p(-jnp.exp2(dec_ref[...]))
        ii = lax.broadcasted_iota(jnp.int32, (c, c), 0).astype(F32)
        jj = lax.broadcasted_iota(jnp.int32, (c, c), 1).astype(F32)
        ir = lax.broadcasted_iota(jnp.int32, (c, dk), 0).astype(F32)
        for h in range(RET_HEADS):
            lf = lg[h:h + 1, :]
            lb = lg[RET_HEADS + h:RET_HEADS + h + 1, :]
            keep_f = ii >= jj
            dm_ref[h] = jnp.where(keep_f, jnp.exp(lf * jnp.where(keep_f, ii - jj, 0.0)), 0.0)
            keep_b = jj > ii
            dm_ref[RET_HEADS + h] = jnp.where(keep_b, jnp.exp(lb * jnp.where(keep_b, jj - ii, 0.0)), 0.0)
            qw_ref[h] = jnp.exp(lf[:, :dk] * (ir + 1.0))
            kw_ref[h] = jnp.exp(lf[:, :dk] * (c - 1.0 - ir))
            qw_ref[RET_HEADS + h] = jnp.exp(lb[:, :dk] * (c - ir))
            kw_ref[RET_HEADS + h] = jnp.exp(lb[:, :dk] * ir)
            gc_ref[h] = jnp.exp(lf[:, :dk] * float(c))
            gc_ref[RET_HEADS + h] = jnp.exp(lb[:, :dk] * float(c))

    def run(src_ref, dst_ref, d):
        for h in range(RET_HEADS):
            r = d * RET_HEADS + h
            q = src_ref[0, :, h * dk:(h + 1) * dk]
            k = src_ref[0, :, 256 + h * dk:256 + (h + 1) * dk] * (RET_DK ** -0.5)
            v = src_ref[0, :, 512 + h * dk:512 + (h + 1) * dk].astype(BF16)
            s_prev = s_ref[r]
            sc = _dot_nt(q.astype(BF16), k.astype(BF16)) * dm_ref[r]
            y = _dot(sc.astype(BF16), v) + _dot((q * qw_ref[r]).astype(BF16), s_prev.astype(BF16))
            s_ref[r] = gc_ref[r] * s_prev + _dot_tn((k * kw_ref[r]).astype(BF16), v)
            dst_ref[0, :, h * dk:(h + 1) * dk] = y

    @pl.when(n < n_ctx)
    def _():
        run(cf_ref, ycf_ref, 0)
        run(cb_ref, ycb_ref, 1)

    @pl.when(n >= n_ctx)
    def _():
        run(xf_ref, yxf_ref, 0)
        run(xb_ref, yxb_ref, 1)


def _retention(ret_c, ret_x, dec):
    b, lc, _ = ret_c.shape
    lx = ret_x.shape[1]
    c = RET_CHUNK
    nc, nx = lc // c, lx // c
    cfi = lambda bi, n: (bi, jnp.minimum(n, nc - 1), 0)
    cbi = lambda bi, n: (bi, nc - 1 - jnp.minimum(n, nc - 1), 0)
    xfi = lambda bi, n: (bi, jnp.maximum(n - nc, 0), 0)
    xbi = lambda bi, n: (bi, nx - 1 - jnp.maximum(n - nc, 0), 0)
    ydim = RET_HEADS * RET_DK
    return pl.pallas_call(
        functools.partial(_ret_kernel, n_ctx=nc),
        out_shape=[jax.ShapeDtypeStruct((b, lc, ydim), F32), jax.ShapeDtypeStruct((b, lc, ydim), F32),
                   jax.ShapeDtypeStruct((b, lx, ydim), F32), jax.ShapeDtypeStruct((b, lx, ydim), F32)],
        grid=(b, nc + nx),
        in_specs=[pl.BlockSpec(dec.shape, lambda bi, n: (0, 0)),
                  pl.BlockSpec((1, c, 768), cfi), pl.BlockSpec((1, c, 768), cbi),
                  pl.BlockSpec((1, c, 768), xfi), pl.BlockSpec((1, c, 768), xbi)],
        out_specs=[pl.BlockSpec((1, c, ydim), cfi), pl.BlockSpec((1, c, ydim), cbi),
                   pl.BlockSpec((1, c, ydim), xfi), pl.BlockSpec((1, c, ydim), xbi)],
        scratch_shapes=[pltpu.VMEM((2 * RET_HEADS, RET_DK, RET_DK), F32),
                        pltpu.VMEM((2 * RET_HEADS, c, c), F32),
                        pltpu.VMEM((2 * RET_HEADS, c, RET_DK), F32),
                        pltpu.VMEM((2 * RET_HEADS, c, RET_DK), F32),
                        pltpu.VMEM((2 * RET_HEADS, 1, RET_DK), F32)],
        compiler_params=_cparams(("arbitrary", "arbitrary")),
        name="retention",
    )(dec, ret_c, ret_c, ret_x, ret_x)


def _mla_kernel(*refs, rope, has_x, tq, tk):
    it = iter(refs)
    cqn_ref, wq_ref = next(it), next(it)
    if rope:
        cq_ref, sq_ref = next(it), next(it)
    kc_ref = next(it)
    if has_x:
        kx_ref = next(it)
    wuv_ref, o_ref, qcat_ref, sc_ref = next(it), next(it), next(it), next(it)
    if has_x:
        sx_ref = next(it)

    nh = MLA_HEADS
    scale = (MLA_NOPE + MLA_ROPE) ** -0.5 * LOG2E
    qa = _dot(cqn_ref[0], wq_ref[...]) * scale
    qr = qa[:, nh * MLA_KV_RANK:nh * MLA_KV_RANK + nh * MLA_ROPE]
    if rope:
        qp = qa[:, nh * MLA_KV_RANK + nh * MLA_ROPE:]
        cq, sq = cq_ref[...], sq_ref[...]
        qr = jnp.concatenate([qr[:, :LANES] * cq + qp[:, :LANES] * sq,
                              qr[:, LANES:] * cq + qp[:, LANES:] * sq], axis=1)
    for h in range(nh):
        qcat_ref[h * tq:(h + 1) * tq, 0:MLA_KV_RANK] = qa[:, h * MLA_KV_RANK:(h + 1) * MLA_KV_RANK].astype(BF16)
        qcat_ref[h * tq:(h + 1) * tq, MLA_KV_RANK:KCAT] = qr[:, h * MLA_ROPE:(h + 1) * MLA_ROPE].astype(BF16)
    q = qcat_ref[...]
    n_pairs = kx_ref.shape[1] // (2 * tk) if has_x else 0

    def tile_max(s):
        return functools.reduce(jnp.maximum, [s[:, j:j + LANES] for j in range(0, s.shape[1], LANES)])

    def kx_chunk(c):
        return kx_ref[0, pl.ds(pl.multiple_of(c * tk, tk), tk), :]

    s = _dot_nt(q, kc_ref[0])
    sc_ref[...] = s
    mx = tile_max(s)

    def pass1(i, mx):
        for u in range(2):
            s = _dot_nt(q, kx_chunk(2 * i + u))
            sx_ref[2 * i + u] = s
            mx = jnp.maximum(mx, tile_max(s))
        return mx

    if n_pairs:
        mx = lax.fori_loop(0, n_pairs, pass1, mx)
    m = jnp.max(mx, axis=-1, keepdims=True)

    def part(s, kblk):
        n = kblk.shape[0]
        ones_col = jnp.where(lax.broadcasted_iota(jnp.int32, (n, LANES), 1) == 0, 1.0, 0.0).astype(BF16)
        v_aug = jnp.concatenate([kblk[:, 0:MLA_KV_RANK], ones_col], axis=1)
        return _dot(jnp.exp2(s - m).astype(BF16), v_aug)

    def pass2(i, acc):
        return acc + (part(sx_ref[2 * i], kx_chunk(2 * i)) + part(sx_ref[2 * i + 1], kx_chunk(2 * i + 1)))

    acc = part(sc_ref[...], kc_ref[0])
    if n_pairs:
        acc = lax.fori_loop(0, n_pairs, pass2, acc)
    o = (acc[:, 0:MLA_KV_RANK] * (1.0 / acc[:, MLA_KV_RANK:MLA_KV_RANK + 1])).astype(BF16)
    for h in range(nh):
        o_ref[0, :, h * MLA_DV:(h + 1) * MLA_DV] = _dot(o[h * tq:(h + 1) * tq, :], wuv_ref[h]).astype(BF16)


def _mla(cqn, wq_all, tables, k_c, k_x, wuv, tq, tk):
    b, l, _ = cqn.shape
    rope = tables is not None
    has_x = k_x is not None
    assert not has_x or k_x.shape[1] % (2 * tk) == 0
    args = [cqn, wq_all]
    specs = [pl.BlockSpec((1, tq, MLA_Q_RANK), lambda bi, i: (bi, i, 0)),
             pl.BlockSpec(wq_all.shape, lambda bi, i: (0, 0))]
    if rope:
        for t in tables:
            args.append(t)
            specs.append(pl.BlockSpec((tq, LANES), lambda bi, i: (i, 0)))
    args.append(k_c)
    specs.append(pl.BlockSpec((1, k_c.shape[1], KCAT), lambda bi, i: (bi, 0, 0)))
    if has_x:
        args.append(k_x)
        specs.append(pl.BlockSpec((1, k_x.shape[1], KCAT), lambda bi, i: (bi, 0, 0)))
    args.append(wuv)
    specs.append(pl.BlockSpec(wuv.shape, lambda bi, i: (0, 0, 0)))
    rows = MLA_HEADS * tq
    scratch = [pltpu.VMEM((rows, KCAT), BF16), pltpu.VMEM((rows, k_c.shape[1]), F32)]
    if has_x:
        scratch.append(pltpu.VMEM((k_x.shape[1] // tk, rows, tk), F32))
    return pl.pallas_call(
        functools.partial(_mla_kernel, rope=rope, has_x=has_x, tq=tq, tk=tk),
        out_shape=jax.ShapeDtypeStruct((b, l, MLA_HEADS * MLA_DV), BF16),
        grid=(b, l // tq), in_specs=specs,
        out_specs=pl.BlockSpec((1, tq, MLA_HEADS * MLA_DV), lambda bi, i: (bi, i, 0)),
        scratch_shapes=scratch,
        compiler_params=_cparams(("arbitrary", "arbitrary")),
        name="mla_attn",
    )(*args)


def _win_kernel(*refs, band, l_x):
    it = iter(refs)
    sink_ref, q_ref = next(it), next(it)
    if band:
        kx_ref, vx_ref = next(it), next(it)
    kc_ref, vc_ref, o_ref = next(it), next(it), next(it)
    w = WINDOW
    d = WIN_DH
    g = WIN_Q_HEADS // WIN_KV_HEADS
    scale = d ** -0.5
    n = pl.program_id(1)
    q = q_ref[0]
    rows = g * w
    row_id = lax.broadcasted_iota(jnp.int32, (rows, 1), 0)
    if band:
        start = pl.multiple_of(jnp.clip((n - 1) * w, 0, l_x - 3 * w), w)
        qi = n * w + lax.broadcasted_iota(jnp.int32, (rows, 3 * w), 0) % w
        kj = start + lax.broadcasted_iota(jnp.int32, (rows, 3 * w), 1)
        keep = jnp.abs(kj - qi) <= WINDOW
        kb_all = kx_ref[0, pl.ds(start, 3 * w), :]
        vb_all = vx_ref[0, pl.ds(start, 3 * w), :]
    kc_all, vc_all = kc_ref[0], vc_ref[0]
    for j in range(WIN_KV_HEADS):
        q2 = jnp.concatenate([q[:, (g * j + t) * d:(g * j + t + 1) * d] for t in range(g)], axis=0)
        sk = jnp.zeros((rows, 1), F32)
        for t in range(g):
            sk = jnp.where(row_id // w == t, sink_ref[g * j + t], sk)
        s_ctx = _dot_nt(q2, kc_all[:, j * d:(j + 1) * d]) * scale
        m = jnp.maximum(jnp.max(s_ctx, axis=-1, keepdims=True), sk)
        if band:
            s_loc = _dot_nt(q2, kb_all[:, j * d:(j + 1) * d]) * scale
            s_loc = jnp.where(keep, s_loc, NEG_INF)
            m = jnp.maximum(m, jnp.max(s_loc, axis=-1, keepdims=True))
        p_ctx = jnp.exp(s_ctx - m)
        l = jnp.sum(p_ctx, axis=-1, keepdims=True) + jnp.exp(sk - m)
        o = _dot(p_ctx.astype(BF16), vc_all[:, j * d:(j + 1) * d])
        if band:
            p_loc = jnp.exp(s_loc - m)
            l = l + jnp.sum(p_loc, axis=-1, keepdims=True)
            o = o + _dot(p_loc.astype(BF16), vb_all[:, j * d:(j + 1) * d])
        o = (o * (1.0 / l)).astype(BF16)
        for t in range(g):
            o_ref[0, :, (g * j + t) * d:(g * j + t + 1) * d] = o[t * w:(t + 1) * w, :]


def _window(win_q, win_x, win_c, sink):
    b, l, _ = win_q.shape
    lc = win_c.shape[1]
    band = win_x is not None
    kvw = WIN_KV_HEADS * WIN_DH
    args = [sink, win_q]
    specs = [pl.BlockSpec(memory_space=pltpu.SMEM),
             pl.BlockSpec((1, WINDOW, 256), lambda bi, n: (bi, n, 0))]
    if band:
        args += [win_x, win_x]
        specs += [pl.BlockSpec((1, l, kvw), lambda bi, n: (bi, 0, 2)),
                  pl.BlockSpec((1, l, kvw), lambda bi, n: (bi, 0, 3))]
    args += [win_c, win_c]
    specs += [pl.BlockSpec((1, lc, kvw), lambda bi, n: (bi, 0, 2)),
              pl.BlockSpec((1, lc, kvw), lambda bi, n: (bi, 0, 3))]
    return pl.pallas_call(
        functools.partial(_win_kernel, band=band, l_x=l),
        out_shape=jax.ShapeDtypeStruct((b, l, WIN_Q_HEADS * WIN_DH), BF16),
        grid=(b, l // WINDOW), in_specs=specs,
        out_specs=pl.BlockSpec((1, WINDOW, WIN_Q_HEADS * WIN_DH), lambda bi, n: (bi, n, 0)),
        compiler_params=_cparams(("arbitrary", "arbitrary")),
        name="window_attn",
    )(*args)


def _outproj_kernel(x_ref, yf_ref, yb_ref, rg_ref, mla_ref, win_ref, g2_ref, sh_ref, sc_ref,
                    n2_ref, wo_ref, rw_ref, rwt_ref, xo_ref, h2_ref, aff_ref, afft_ref):
    y = yf_ref[0] + yb_ref[0]
    parts = []
    for h in range(RET_HEADS):
        yh = y[:, h * RET_DK:(h + 1) * RET_DK]
        mu = jnp.mean(yh, axis=-1, keepdims=True)
        var = jnp.mean(jnp.square(yh - mu), axis=-1, keepdims=True)
        parts.append((yh - mu) * lax.rsqrt(var + GN_EPS))
    ret = jnp.concatenate(parts, axis=1) * _silu(rg_ref[0])
    proj = (_dot(ret.astype(BF16), wo_ref[0:256, :]) + _dot(mla_ref[0], wo_ref[256:768, :])
            + _dot(win_ref[0], wo_ref[768:1024, :]))
    x = x_ref[0] + g2_ref[0] * proj
    xo_ref[0] = x
    h2 = _rms(x, n2_ref[...]) * (1.0 + sc_ref[0]) + sh_ref[0]
    h2_ref[0] = h2
    logits = jnp.dot(h2, rw_ref[...], preferred_element_type=F32, precision=lax.Precision.HIGHEST)
    e = jnp.exp(logits - jnp.max(logits, axis=-1, keepdims=True))
    aff_ref[0] = e / jnp.sum(e, axis=-1, keepdims=True)
    lt = lax.dot_general(rwt_ref[...], h2, (((1,), (1,)), ((), ())), preferred_element_type=F32,
                         precision=lax.Precision.HIGHEST)
    et = jnp.exp(lt - jnp.max(lt, axis=0, keepdims=True))
    afft_ref[0] = et / jnp.sum(et, axis=0, keepdims=True)


def _outproj(x, yf, yb, ret, mla, win, mod, mrow, layer, n2, w_out, rw, rwt, tm):
    b, l, d = x.shape

    def mspec(k):
        return pl.BlockSpec((1, 1, d), lambda bi, i: ((layer * MOD_ROWS + mrow(bi)) * 6 + k, 0, 0))

    tok = lambda w: pl.BlockSpec((1, tm, w), lambda bi, i: (bi, i, 0))
    const = lambda a: pl.BlockSpec(a.shape, lambda bi, i: (0,) * a.ndim)
    return pl.pallas_call(
        _outproj_kernel,
        out_shape=[jax.ShapeDtypeStruct((b, l, d), F32), jax.ShapeDtypeStruct((b, l, d), F32),
                   jax.ShapeDtypeStruct((b, l, N_EXPERTS), F32), jax.ShapeDtypeStruct((b, N_EXPERTS, l), F32)],
        grid=(b, l // tm),
        in_specs=[tok(d), tok(256), tok(256), pl.BlockSpec((1, tm, 256), lambda bi, i: (bi, i, 3)),
                  tok(512), tok(256), mspec(2), mspec(3), mspec(4), const(n2), const(w_out),
                  const(rw), const(rwt)],
        out_specs=[tok(d), tok(d), tok(N_EXPERTS), pl.BlockSpec((1, N_EXPERTS, tm), lambda bi, i: (bi, 0, i))],
        compiler_params=_cparams(("arbitrary", "arbitrary")),
        name="outproj",
    )(x, yf, yb, ret, mla, win, mod, mod, mod, n2, w_out, rw, rwt)


def _route_kernel(afft_ref, tri_ref, tt_ref, idx_ref, cum_ref, *, cap):
    a = afft_ref[0]
    ne, l = a.shape
    as_f32 = lambda bits: lax.bitcast_convert_type(bits, F32)

    def search(_, c):
        lo, hi = c
        mid = lo + ((hi - lo) >> 1)
        cnt = jnp.sum(jnp.where(a >= as_f32(mid), 1.0, 0.0), axis=1, keepdims=True)
        ok = cnt >= cap
        return jnp.where(ok, mid, lo), jnp.where(ok, hi, mid)

    lo, hi = lax.fori_loop(0, 31, search, (jnp.zeros((ne, 1), jnp.int32),
                                           jnp.full((ne, 1), 0x7F800000, jnp.int32)))
    ge = jnp.where(a >= as_f32(lo), 1.0, 0.0)
    gt = jnp.where(a >= as_f32(hi), 1.0, 0.0)
    eq = ge - gt
    need = cap - jnp.sum(gt, axis=1, keepdims=True)

    def cumsum_to_scratch(mask):
        carry = jnp.zeros((ne, 1), F32)
        for j in range(l // LANES):
            cj = _dot(mask[:, j * LANES:(j + 1) * LANES].astype(BF16), tri_ref[...]) + carry
            cum_ref[:, j * LANES:(j + 1) * LANES] = cj
            carry = cj[:, LANES - 1:LANES]

    cumsum_to_scratch(eq)
    sel = gt + eq * jnp.where(cum_ref[...] - eq < need, 1.0, 0.0)
    cumsum_to_scratch(sel)
    cum_ref[...] = sel * cum_ref[...]
    slot = (lax.broadcasted_iota(jnp.int32, (cap, l), 0) + 1).astype(F32)

    def per_expert(e, _):
        onehot = jnp.where(cum_ref[pl.ds(e, 1), :] == slot, 1.0, 0.0).astype(BF16)
        r = _dot_nt(tt_ref[...], onehot)
        idx_ref[0, pl.ds(e, 1), :, :] = (r[0:1, :] * LANES + r[1:2, :]).astype(jnp.int32).reshape(1, 1, cap)
        return 0

    lax.fori_loop(0, ne, per_expert, 0)


def _route(afft, cap):
    b, ne, l = afft.shape
    u = np.arange(LANES)
    tri = jnp.asarray(u[:, None] <= u[None, :], BF16)
    t = np.arange(l)
    tt = np.zeros((8, l), np.float32)
    tt[0], tt[1] = t // LANES, t % LANES
    tt = jnp.asarray(tt, BF16)
    return pl.pallas_call(
        functools.partial(_route_kernel, cap=cap),
        out_shape=jax.ShapeDtypeStruct((b, ne, 1, cap), jnp.int32),
        grid=(b,),
        in_specs=[pl.BlockSpec((1, ne, l), lambda bi: (bi, 0, 0)),
                  pl.BlockSpec(tri.shape, lambda bi: (0, 0)), pl.BlockSpec(tt.shape, lambda bi: (0, 0))],
        out_specs=pl.BlockSpec((1, ne, 1, cap), lambda bi: (bi, 0, 0, 0)),
        scratch_shapes=[pltpu.VMEM((ne, l), F32)],
        compiler_params=_cparams(("arbitrary",)),
        name="route",
    )(afft, tri, tt)


def _ffn_kernel(idx_ref, h_ref, aff_ref, wg_ref, wu_ref, wd_ref, ys_ref, xs_ref, g_ref, *, nb, cap):
    e = pl.program_id(1)
    for s in range(nb):
        def gather(j, _):
            t = idx_ref[s, 0, 0, j]
            xs_ref[pl.ds(s * cap + j, 1), :] = h_ref[s, pl.ds(t, 1), :]
            g_ref[pl.ds(s * cap + j, 1), :] = aff_ref[s, pl.ds(t, 1), :]
            return 0
        lax.fori_loop(0, cap, gather, 0, unroll=8)
    xb = xs_ref[...].astype(BF16)
    hid = _silu(_dot(xb, wg_ref[0])) * _dot(xb, wu_ref[0])
    y = _dot(hid.astype(BF16), wd_ref[0])
    lane = lax.broadcasted_iota(jnp.int32, g_ref.shape, 1)
    gate = jnp.sum(jnp.where(lane == e, g_ref[...], 0.0), axis=1, keepdims=True)
    y = y * gate
    for s in range(nb):
        ys_ref[s, 0] = y[s * cap:(s + 1) * cap, :]


def _expert_ffn(idx, h2, aff, wg, wu, wd, nb):
    b, l, d = h2.shape
    ne, cap = idx.shape[1], idx.shape[3]
    ff = wg.shape[2]
    return pl.pallas_call(
        functools.partial(_ffn_kernel, nb=nb, cap=cap),
        out_shape=jax.ShapeDtypeStruct((b, ne, cap, d), F32),
        grid=(b // nb, ne),
        in_specs=[pl.BlockSpec((nb, 1, 1, cap), lambda bi, e: (bi, e, 0, 0), memory_space=pltpu.SMEM),
                  pl.BlockSpec((nb, l, d), lambda bi, e: (bi, 0, 0), pipeline_mode=pl.Buffered(1)),
                  pl.BlockSpec((nb, l, ne), lambda bi, e: (bi, 0, 0)),
                  pl.BlockSpec((1, d, ff), lambda bi, e: (e, 0, 0)),
                  pl.BlockSpec((1, d, ff), lambda bi, e: (e, 0, 0)),
                  pl.BlockSpec((1, ff, d), lambda bi, e: (e, 0, 0))],
        out_specs=pl.BlockSpec((nb, 1, cap, d), lambda bi, e: (bi, e, 0, 0)),
        scratch_shapes=[pltpu.VMEM((nb * cap, d), F32), pltpu.VMEM((nb * cap, ne), F32)],
        compiler_params=_cparams(("arbitrary", "arbitrary")),
        name="expert_ffn",
    )(idx, h2, aff, wg, wu, wd)


def _scatter_kernel(idx_ref, ys_ref, acc_ref, *, nb, cap):
    @pl.when(pl.program_id(1) == 0)
    def _():
        acc_ref[...] = jnp.zeros_like(acc_ref)

    for s in range(nb):
        def add(j, _):
            t = idx_ref[s, 0, 0, j]
            acc_ref[s, pl.ds(t, 1), :] = acc_ref[s, pl.ds(t, 1), :] + ys_ref[s, 0, pl.ds(j, 1), :]
            return 0
        lax.fori_loop(0, cap, add, 0, unroll=8)


def _scatter(idx, ys, l, nb):
    b, ne, cap, d = ys.shape
    return pl.pallas_call(
        functools.partial(_scatter_kernel, nb=nb, cap=cap),
        out_shape=jax.ShapeDtypeStruct((b, l, d), F32),
        grid=(b // nb, ne),
        in_specs=[pl.BlockSpec((nb, 1, 1, cap), lambda bi, e: (bi, e, 0, 0), memory_space=pltpu.SMEM),
                  pl.BlockSpec((nb, 1, cap, d), lambda bi, e: (bi, e, 0, 0))],
        out_specs=pl.BlockSpec((nb, l, d), lambda bi, e: (bi, 0, 0)),
        compiler_params=_cparams(("arbitrary", "arbitrary")),
        name="moe_scatter",
    )(idx, ys)


def _final_kernel(x_ref, acc_ref, g5_ref, fg_ref, o_ref):
    o_ref[0] = _rms(x_ref[0] + g5_ref[0] * acc_ref[0], fg_ref[...])


def _final(x, acc, mod, layer, fg, tm):
    b, l, d = x.shape
    tok = pl.BlockSpec((1, tm, d), lambda bi, i: (bi, i, 0))
    return pl.pallas_call(
        _final_kernel,
        out_shape=jax.ShapeDtypeStruct((b, l, d), F32),
        grid=(b, l // tm),
        in_specs=[tok, tok, pl.BlockSpec((1, 1, d), lambda bi, i: ((layer * MOD_ROWS + bi) * 6 + 5, 0, 0)),
                  pl.BlockSpec(fg.shape, lambda bi, i: (0, 0))],
        out_specs=tok,
        compiler_params=_cparams(("arbitrary", "arbitrary")),
        name="final_norm",
    )(x, acc, mod, fg)


def _partner(dh):
    q = dh // 4
    return np.concatenate([np.arange(q, 2 * q), np.arange(0, q), np.arange(3 * q, 4 * q), np.arange(2 * q, 3 * q)])


def _rope_tables(n_tok, dh, reps):
    rows = n_tok // GRID_W
    row = jnp.repeat(jnp.arange(rows, dtype=jnp.int32), GRID_W)
    col = jnp.tile(jnp.arange(GRID_W, dtype=jnp.int32), rows)
    half = dh // 2
    inv = ROPE_BASE ** (-jnp.arange(0, half, 2, dtype=jnp.float32) / half)
    ar = row.astype(jnp.float32)[:, None] * inv[None, :]
    ac = col.astype(jnp.float32)[:, None] * inv[None, :]
    cos = jnp.concatenate([jnp.cos(ar), jnp.cos(ar), jnp.cos(ac), jnp.cos(ac)], axis=1)
    sin = jnp.concatenate([-jnp.sin(ar), jnp.sin(ar), -jnp.sin(ac), jnp.sin(ac)], axis=1)
    return jnp.tile(cos, (1, reps)), jnp.tile(sin, (1, reps))


def _extend_w_in(w):
    d = w.shape[0]
    wq, wk, wv = w[:, 1440:1696], w[:, 1696:1824], w[:, 1824:1952]
    kr = w[:, 1408:1440]
    pq = np.concatenate([h * WIN_DH + _partner(WIN_DH) for h in range(WIN_Q_HEADS)])
    pk = np.concatenate([h * WIN_DH + _partner(WIN_DH) for h in range(WIN_KV_HEADS)])
    ext = jnp.concatenate([w[:, 0:1408], wq, wq[:, pq], wk, wk[:, pk], wv, kr, kr[:, _partner(MLA_ROPE)],
                           jnp.zeros((d, N_EXT - C_KR - 2 * MLA_ROPE), w.dtype)], axis=1)
    return ext.astype(BF16)


def kernel(x, c, ctx, c_ctx, norm1_g, norm2_g, ada_w, ada_b, w_in, ret_decay_f, ret_decay_b,
           mla_qnorm_g, mla_kvnorm_g, mla_w_uq, mla_w_uk, mla_w_uv, win_sink, w_out,
           router_w, exp_w_gate, exp_w_up, exp_w_down, final_g):
    b, l, d = x.shape
    lc = ctx.shape[1]
    depth = w_in.shape[0]
    assert b + 1 <= MOD_ROWS and l % 512 == 0 and lc % RET_CHUNK == 0 and l >= 3 * WINDOW

    c_all = jnp.concatenate([c, c_ctx[None, :], jnp.zeros((MOD_ROWS - b - 1, d), F32)], axis=0)
    mod = _modulation(c_all, ada_w, ada_b).reshape(depth * MOD_ROWS * 6, 1, d)
    row_x = lambda bi: bi
    row_c = lambda bi: b

    wabs = _absorbed_q_weights(mla_w_uq, mla_w_uk)
    cw, sw = _rope_tables(l, WIN_DH, LANES // WIN_DH)
    cm, sm = _rope_tables(l, MLA_ROPE, 1)
    cq, sq = _rope_tables(l, MLA_ROPE, LANES // MLA_ROPE)
    pr = np.concatenate([h * MLA_ROPE + _partner(MLA_ROPE) for h in range(MLA_HEADS)])

    tm = 512
    tq = 128
    tk = 1024 if l % 2048 == 0 else l // 2
    acc_x = acc_c = None
    for layer in range(depth):
        need_ctx = layer < depth - 1
        w_ext = _extend_w_in(w_in[layer])
        g1 = norm1_g[layer].reshape(1, d)
        qng = mla_qnorm_g[layer].reshape(1, -1)
        kvg = mla_kvnorm_g[layer].reshape(1, -1)
        uq_rope = mla_w_uq[layer][:, :, MLA_NOPE:].reshape(MLA_Q_RANK, MLA_HEADS * MLA_ROPE)
        wq_all = jnp.concatenate([wabs[layer], uq_rope, uq_rope[:, pr]], axis=1).astype(BF16)
        wuv = jnp.transpose(mla_w_uv[layer], (1, 0, 2)).astype(BF16)
        dec = jnp.broadcast_to(jnp.concatenate([ret_decay_f[layer], ret_decay_b[layer]])[:, None],
                               (2 * RET_HEADS, LANES)).astype(F32)
        wo = w_out[layer].astype(BF16)
        n2 = norm2_g[layer].reshape(1, d)
        rw = router_w[layer]
        rwt = rw.T
        wg, wu, wd = (exp_w_gate[layer].astype(BF16), exp_w_up[layer].astype(BF16),
                      exp_w_down[layer].astype(BF16))

        res_x = None if layer == 0 else (acc_x, layer - 1)
        res_c = None if layer == 0 else (acc_c, layer - 1)
        x, ret_x, cqn_x, kcat_x, win_x = _inproj(x, mod, row_x, layer, g1, w_ext, qng, kvg,
                                                 (cw, sw, cm, sm), res_x, tm)
        ctx, ret_c, cqn_c, kcat_c, win_c = _inproj(ctx, mod, row_c, layer, g1, w_ext, qng, kvg,
                                                   None, res_c, min(tm, lc))
        ycf, ycb, yxf, yxb = _retention(ret_c, ret_x, dec)
        mla_x = _mla(cqn_x, wq_all, (cq, sq), kcat_c, kcat_x, wuv, tq, tk)
        wn_x = _window(win_x, win_x, win_c, win_sink[layer])
        x, h2x, aff_x, afft_x = _outproj(x, yxf, yxb, ret_x, mla_x, wn_x, mod, row_x, layer, n2, wo, rw, rwt, tm)
        idx_x = _route(afft_x, CAPACITY_FACTOR * l // N_EXPERTS)
        acc_x = _scatter(idx_x, _expert_ffn(idx_x, h2x, aff_x, wg, wu, wd, 1), l, 1)
        if need_ctx:
            mla_c = _mla(cqn_c, wq_all, None, kcat_c, None, wuv, min(tq, lc), tk)
            wn_c = _window(win_c, None, win_c, win_sink[layer])
            ctx, h2c, aff_c, afft_c = _outproj(ctx, ycf, ycb, ret_c, mla_c, wn_c, mod, row_c, layer, n2, wo,
                                               rw, rwt, min(tm, lc))
            idx_c = _route(afft_c, CAPACITY_FACTOR * lc // N_EXPERTS)
            acc_c = _scatter(idx_c, _expert_ffn(idx_c, h2c, aff_c, wg, wu, wd, b), lc, b)
    return _final(x, acc_x, mod, depth - 1, final_g.reshape(1, d), tm)
```

```python
import functools

import numpy as np
import jax
import jax.numpy as jnp
from jax import lax
from jax.experimental import pallas as pl
from jax.experimental.pallas import tpu as pltpu

F32 = jnp.float32
BF16 = jnp.bfloat16

D_MODEL = 1024
DEPTH = 2
GRID_W = 64
RET_HEADS = 4
RET_DK = 64
RET_CHUNK = 128
MLA_HEADS = 8
MLA_Q_RANK = 256
MLA_KV_RANK = 128
MLA_NOPE = 64
MLA_ROPE = 32
MLA_DV = 64
WIN_Q_HEADS = 4
WIN_KV_HEADS = 2
WIN_DH = 64
WINDOW = 128
N_EXPERTS = 16
EXPERT_FF = 768
CAPACITY_FACTOR = 2
ROPE_BASE = 10000.0
NORM_EPS = 1e-6
GN_EPS = 1e-5
NEG_INF = -1e30
LOG2E = 1.4426950408889634

LANES = 128
MOD_ROWS = 16
KCAT = MLA_KV_RANK + MLA_ROPE
VMEM_LIMIT = 56 * 1024 * 1024

C_RET = 0
C_CQ = 1024
C_CKV = 1280
C_WQ = 1408
C_WQP = 1664
C_WK = 1920
C_WKP = 2048
C_WV = 2176
C_KR = 2304
N_EXT = 2432


def _cparams(sem):
    return pltpu.CompilerParams(dimension_semantics=sem, vmem_limit_bytes=VMEM_LIMIT)


def _dot(a, b):
    return jnp.dot(a, b, preferred_element_type=F32)


def _dot_nt(a, b):
    return lax.dot_general(a, b, (((1,), (1,)), ((), ())), preferred_element_type=F32)


def _dot_tn(a, b):
    return lax.dot_general(a, b, (((0,), (0,)), ((), ())), preferred_element_type=F32)


def _rms(x, g):
    return x * lax.rsqrt(jnp.mean(x * x, axis=-1, keepdims=True) + NORM_EPS) * g


def _silu(x):
    return x * jax.nn.sigmoid(x)


def _mod_kernel(c_ref, w_ref, b_ref, o_ref):
    o_ref[0] = jnp.dot(_silu(c_ref[...]), w_ref[0], preferred_element_type=F32,
                       precision=lax.Precision.HIGHEST) + b_ref[0]


def _modulation(c_all, ada_w, ada_b):
    depth, d, n = ada_w.shape
    tn = 1024
    return pl.pallas_call(
        _mod_kernel,
        out_shape=jax.ShapeDtypeStruct((depth, MOD_ROWS, n), F32),
        grid=(depth, n // tn),
        in_specs=[pl.BlockSpec((MOD_ROWS, d), lambda l, j: (0, 0)),
                  pl.BlockSpec((1, d, tn), lambda l, j: (l, 0, j)),
                  pl.BlockSpec((1, 1, tn), lambda l, j: (l, 0, j))],
        out_specs=pl.BlockSpec((1, MOD_ROWS, tn), lambda l, j: (l, 0, j)),
        compiler_params=_cparams(("arbitrary", "arbitrary")),
        name="adaln_mod",
    )(c_all, ada_w, ada_b.reshape(depth, 1, n))


def _wabs_kernel(uq_ref, uk_ref, o_ref):
    o_ref[0] = jnp.dot(uq_ref[0, 0], uk_ref[0, 0], preferred_element_type=F32,
                       precision=lax.Precision.HIGHEST)


def _absorbed_q_weights(w_uq, w_uk):
    depth = w_uq.shape[0]
    uq = jnp.transpose(w_uq[..., :MLA_NOPE], (0, 2, 1, 3))
    uk = jnp.transpose(w_uk, (0, 2, 3, 1))
    return pl.pallas_call(
        _wabs_kernel,
        out_shape=jax.ShapeDtypeStruct((depth, MLA_Q_RANK, MLA_HEADS * MLA_KV_RANK), F32),
        grid=(depth, MLA_HEADS),
        in_specs=[pl.BlockSpec((1, 1, MLA_Q_RANK, MLA_NOPE), lambda l, h: (l, h, 0, 0)),
                  pl.BlockSpec((1, 1, MLA_NOPE, MLA_KV_RANK), lambda l, h: (l, h, 0, 0))],
        out_specs=pl.BlockSpec((1, MLA_Q_RANK, MLA_KV_RANK), lambda l, h: (l, 0, h)),
        compiler_params=_cparams(("arbitrary", "arbitrary")),
        name="mla_absorb",
    )(uq, uk)


def _inproj_kernel(*refs, rope, resid):
    it = iter(refs)
    x_ref = next(it)
    if resid:
        acc_ref, g5_ref = next(it), next(it)
    g1_ref, sh_ref, sc_ref, w_ref, qng_ref, kvg_ref = (next(it) for _ in range(6))
    if rope:
        cw_ref, sw_ref, cm_ref, sm_ref = (next(it) for _ in range(4))
    if resid:
        xo_ref = next(it)
    ret_ref, cqn_ref, kcat_ref, win_ref = (next(it) for _ in range(4))

    x = x_ref[0]
    if resid:
        x = x + g5_ref[0] * acc_ref[0]
        xo_ref[0] = x
    h = _rms(x, g1_ref[...]) * (1.0 + sc_ref[0]) + sh_ref[0]
    p = _dot(h.astype(BF16), w_ref[...])

    ret_ref[0] = p[:, C_RET:C_CQ]
    cqn_ref[0] = _rms(p[:, C_CQ:C_CKV], qng_ref[...]).astype(BF16)
    kvn = _rms(p[:, C_CKV:C_WQ], kvg_ref[...])
    wq, wk, wv = p[:, C_WQ:C_WQP], p[:, C_WK:C_WKP], p[:, C_WV:C_KR]
    kr = p[:, C_KR:C_KR + MLA_ROPE]
    if rope:
        wqp, wkp = p[:, C_WQP:C_WK], p[:, C_WKP:C_WV]
        krp = p[:, C_KR + MLA_ROPE:C_KR + 2 * MLA_ROPE]
        cw, sw = cw_ref[...], sw_ref[...]
        wq = jnp.concatenate([wq[:, :LANES] * cw + wqp[:, :LANES] * sw,
                              wq[:, LANES:] * cw + wqp[:, LANES:] * sw], axis=1)
        wk = wk * cw + wkp * sw
        kr = kr * cm_ref[...] + krp * sm_ref[...]
    kcat_ref[0, :, 0:MLA_KV_RANK] = kvn.astype(BF16)
    kcat_ref[0, :, MLA_KV_RANK:KCAT] = kr.astype(BF16)
    win_ref[0, :, 0:256] = wq.astype(BF16)
    win_ref[0, :, 256:384] = wk.astype(BF16)
    win_ref[0, :, 384:512] = wv.astype(BF16)


def _inproj(x, mod, mrow, layer, g1, w_ext, qng, kvg, tables, resid, tm):
    b, l, d = x.shape
    rope = tables is not None

    def mspec(k, lay):
        return pl.BlockSpec((1, 1, d), lambda bi, i: ((lay * MOD_ROWS + mrow(bi)) * 6 + k, 0, 0))

    tok = lambda w: pl.BlockSpec((1, tm, w), lambda bi, i: (bi, i, 0))
    const = lambda a: pl.BlockSpec(a.shape, lambda bi, i: (0,) * a.ndim)
    args, specs = [x], [tok(d)]
    if resid is not None:
        args += [resid[0], mod]
        specs += [tok(d), mspec(5, resid[1])]
    args += [g1, mod, mod, w_ext, qng, kvg]
    specs += [const(g1), mspec(0, layer), mspec(1, layer), const(w_ext), const(qng), const(kvg)]
    if rope:
        for t in tables:
            args.append(t)
            specs.append(pl.BlockSpec((tm, t.shape[1]), lambda bi, i: (i, 0)))
    out_shape, out_specs = [], []
    if resid is not None:
        out_shape.append(jax.ShapeDtypeStruct((b, l, d), F32))
        out_specs.append(tok(d))
    out_shape += [jax.ShapeDtypeStruct((b, l, 1024), F32), jax.ShapeDtypeStruct((b, l, MLA_Q_RANK), BF16),
                  jax.ShapeDtypeStruct((b, l, KCAT), BF16), jax.ShapeDtypeStruct((b, l, 512), BF16)]
    out_specs += [tok(1024), tok(MLA_Q_RANK), tok(KCAT), tok(512)]
    outs = pl.pallas_call(
        functools.partial(_inproj_kernel, rope=rope, resid=resid is not None),
        out_shape=out_shape, grid=(b, l // tm), in_specs=specs, out_specs=out_specs,
        compiler_params=_cparams(("arbitrary", "arbitrary")),
        name="inproj",
    )(*args)
    if resid is None:
        return (x,) + tuple(outs)
    return tuple(outs)


def _ret_kernel(dec_ref, c_ref, xf_ref, xb_ref, ycf_ref, ycb_ref, yxf_ref, yxb_ref,
                s_ref, dm_ref, qw_ref, kw_ref, gc_ref):
    n = pl.program_id(1)
    c = RET_CHUNK
    dk = RET_DK

    @pl.when(n == 0)
    def _():
        s_ref[...] = jnp.zeros_like(s_ref)
        lg = jnp.log1p(-jnp.exp2(dec_ref[...]))
        ii = lax.broadcasted_iota(jnp.int32, (c, c), 0).astype(F32)
        jj = lax.broadcasted_iota(jnp.int32, (c, c), 1).astype(F32)
        ir = lax.broadcasted_iota(jnp.int32, (c, dk), 0).astype(F32)
        for h in range(RET_HEADS):
            lf = lg[h:h + 1, :]
            lb = lg[RET_HEADS + h:RET_HEADS + h + 1, :]
            keep_f = ii >= jj
            dm_ref[h] = jnp.where(keep_f, jnp.exp(lf * jnp.where(keep_f, ii - jj, 0.0)), 0.0)
            keep_b = jj > ii
            dm_ref[RET_HEADS + h] = jnp.where(keep_b, jnp.exp(lb * jnp.where(keep_b, jj - ii, 0.0)), 0.0)
            qw_ref[h] = jnp.exp(lf[:, :dk] * (ir + 1.0))
            kw_ref[h] = jnp.exp(lf[:, :dk] * (c - 1.0 - ir))
            qw_ref[RET_HEADS + h] = jnp.exp(lb[:, :dk] * (c - ir))
            kw_ref[RET_HEADS + h] = jnp.exp(lb[:, :dk] * ir)
            gc_ref[h] = jnp.exp(lf[:, :dk] * float(c))
            gc_ref[RET_HEADS + h] = jnp.exp(lb[:, :dk] * float(c))

    def run(src_ref, dst_ref, d):
        n_sub = src_ref.shape[1] // c
        order = range(n_sub) if d == 0 else range(n_sub - 1, -1, -1)
        for h in range(RET_HEADS):
            r = d * RET_HEADS + h
            intra, kv, qd = {}, {}, {}
            for g in order:
                rows = slice(g * c, (g + 1) * c)
                q = src_ref[0, rows, h * dk:(h + 1) * dk]
                k = src_ref[0, rows, 256 + h * dk:256 + (h + 1) * dk] * (RET_DK ** -0.5)
                v = src_ref[0, rows, 512 + h * dk:512 + (h + 1) * dk].astype(BF16)
                sc = _dot_nt(q.astype(BF16), k.astype(BF16)) * dm_ref[r]
                intra[g] = _dot(sc.astype(BF16), v)
                kv[g] = _dot_tn((k * kw_ref[r]).astype(BF16), v)
                qd[g] = (q * qw_ref[r]).astype(BF16)
            s = s_ref[r]
            for g in order:
                dst_ref[0, g * c:(g + 1) * c, h * dk:(h + 1) * dk] = intra[g] + _dot(qd[g], s.astype(BF16))
                s = gc_ref[r] * s + kv[g]
            s_ref[r] = s

    @pl.when(n == 0)
    def _():
        run(c_ref, ycf_ref, 0)
        run(c_ref, ycb_ref, 1)

    @pl.when(n > 0)
    def _():
        run(xf_ref, yxf_ref, 0)
        run(xb_ref, yxb_ref, 1)


def _retention(ret_c, ret_x, dec, g_sub):
    b, lc, _ = ret_c.shape
    lx = ret_x.shape[1]
    c = RET_CHUNK
    blk = g_sub * c
    assert lc % c == 0 and lx % blk == 0
    nx = lx // blk
    ci = lambda bi, n: (bi, 0, 0)
    xfi = lambda bi, n: (bi, jnp.maximum(n - 1, 0), 0)
    xbi = lambda bi, n: (bi, nx - 1 - jnp.maximum(n - 1, 0), 0)
    ydim = RET_HEADS * RET_DK
    return pl.pallas_call(
        _ret_kernel,
        out_shape=[jax.ShapeDtypeStruct((b, lc, ydim), F32), jax.ShapeDtypeStruct((b, lc, ydim), F32),
                   jax.ShapeDtypeStruct((b, lx, ydim), F32), jax.ShapeDtypeStruct((b, lx, ydim), F32)],
        grid=(b, 1 + nx),
        in_specs=[pl.BlockSpec(dec.shape, lambda bi, n: (0, 0)),
                  pl.BlockSpec((1, lc, 768), ci),
                  pl.BlockSpec((1, blk, 768), xfi), pl.BlockSpec((1, blk, 768), xbi)],
        out_specs=[pl.BlockSpec((1, lc, ydim), ci), pl.BlockSpec((1, lc, ydim), ci),
                   pl.BlockSpec((1, blk, ydim), xfi), pl.BlockSpec((1, blk, ydim), xbi)],
        scratch_shapes=[pltpu.VMEM((2 * RET_HEADS, RET_DK, RET_DK), F32),
                        pltpu.VMEM((2 * RET_HEADS, c, c), F32),
                        pltpu.VMEM((2 * RET_HEADS, c, RET_DK), F32),
                        pltpu.VMEM((2 * RET_HEADS, c, RET_DK), F32),
                        pltpu.VMEM((2 * RET_HEADS, 1, RET_DK), F32)],
        compiler_params=_cparams(("arbitrary", "arbitrary")),
        name="retention",
    )(dec, ret_c, ret_x, ret_x)


def _mla_kernel(*refs, rope, has_x, tq, tk):
    it = iter(refs)
    cqn_ref, wq_ref = next(it), next(it)
    if rope:
        cq_ref, sq_ref = next(it), next(it)
    kc_ref = next(it)
    if has_x:
        kx_ref = next(it)
    wuv_ref, o_ref, qcat_ref, sc_ref = next(it), next(it), next(it), next(it)
    if has_x:
        sx_ref = next(it)

    nh = MLA_HEADS
    scale = (MLA_NOPE + MLA_ROPE) ** -0.5 * LOG2E
    qa = _dot(cqn_ref[0], wq_ref[...]) * scale
    qr = qa[:, nh * MLA_KV_RANK:nh * MLA_KV_RANK + nh * MLA_ROPE]
    if rope:
        qp = qa[:, nh * MLA_KV_RANK + nh * MLA_ROPE:]
        cq, sq = cq_ref[...], sq_ref[...]
        qr = jnp.concatenate([qr[:, :LANES] * cq + qp[:, :LANES] * sq,
                              qr[:, LANES:] * cq + qp[:, LANES:] * sq], axis=1)
    for h in range(nh):
        qcat_ref[h * tq:(h + 1) * tq, 0:MLA_KV_RANK] = qa[:, h * MLA_KV_RANK:(h + 1) * MLA_KV_RANK].astype(BF16)
        qcat_ref[h * tq:(h + 1) * tq, MLA_KV_RANK:KCAT] = qr[:, h * MLA_ROPE:(h + 1) * MLA_ROPE].astype(BF16)
    q = qcat_ref[...]
    n_pairs = kx_ref.shape[1] // (2 * tk) if has_x else 0

    def tile_max(s):
        return functools.reduce(jnp.maximum, [s[:, j:j + LANES] for j in range(0, s.shape[1], LANES)])

    def kx_chunk(c):
        return kx_ref[0, pl.ds(pl.multiple_of(c * tk, tk), tk), :]

    s = _dot_nt(q, kc_ref[0])
    sc_ref[...] = s
    mx = tile_max(s)

    def pass1(i, mx):
        for u in range(2):
            s = _dot_nt(q, kx_chunk(2 * i + u))
            sx_ref[2 * i + u] = s
            mx = jnp.maximum(mx, tile_max(s))
        return mx

    if n_pairs:
        mx = lax.fori_loop(0, n_pairs, pass1, mx)
    m = jnp.max(mx, axis=-1, keepdims=True)

    def part(s, kblk):
        n = kblk.shape[0]
        ones_col = jnp.where(lax.broadcasted_iota(jnp.int32, (n, LANES), 1) == 0, 1.0, 0.0).astype(BF16)
        v_aug = jnp.concatenate([kblk[:, 0:MLA_KV_RANK], ones_col], axis=1)
        return _dot(jnp.exp2(s - m).astype(BF16), v_aug)

    def pass2(i, acc):
        return acc + (part(sx_ref[2 * i], kx_chunk(2 * i)) + part(sx_ref[2 * i + 1], kx_chunk(2 * i + 1)))

    acc = part(sc_ref[...], kc_ref[0])
    if n_pairs:
        acc = lax.fori_loop(0, n_pairs, pass2, acc)
    o = (acc[:, 0:MLA_KV_RANK] * (1.0 / acc[:, MLA_KV_RANK:MLA_KV_RANK + 1])).astype(BF16)
    for h in range(nh):
        o_ref[0, :, h * MLA_DV:(h + 1) * MLA_DV] = _dot(o[h * tq:(h + 1) * tq, :], wuv_ref[h]).astype(BF16)


def _mla(cqn, wq_all, tables, k_c, k_x, wuv, tq, tk):
    b, l, _ = cqn.shape
    rope = tables is not None
    has_x = k_x is not None
    assert not has_x or k_x.shape[1] % (2 * tk) == 0
    args = [cqn, wq_all]
    specs = [pl.BlockSpec((1, tq, MLA_Q_RANK), lambda bi, i: (bi, i, 0)),
             pl.BlockSpec(wq_all.shape, lambda bi, i: (0, 0))]
    if rope:
        for t in tables:
            args.append(t)
            specs.append(pl.BlockSpec((tq, LANES), lambda bi, i: (i, 0)))
    args.append(k_c)
    specs.append(pl.BlockSpec((1, k_c.shape[1], KCAT), lambda bi, i: (bi, 0, 0)))
    if has_x:
        args.append(k_x)
        specs.append(pl.BlockSpec((1, k_x.shape[1], KCAT), lambda bi, i: (bi, 0, 0)))
    args.append(wuv)
    specs.append(pl.BlockSpec(wuv.shape, lambda bi, i: (0, 0, 0)))
    rows = MLA_HEADS * tq
    scratch = [pltpu.VMEM((rows, KCAT), BF16), pltpu.VMEM((rows, k_c.shape[1]), F32)]
    if has_x:
        scratch.append(pltpu.VMEM((k_x.shape[1] // tk, rows, tk), F32))
    return pl.pallas_call(
        functools.partial(_mla_kernel, rope=rope, has_x=has_x, tq=tq, tk=tk),
        out_shape=jax.ShapeDtypeStruct((b, l, MLA_HEADS * MLA_DV), BF16),
        grid=(b, l // tq), in_specs=specs,
        out_specs=pl.BlockSpec((1, tq, MLA_HEADS * MLA_DV), lambda bi, i: (bi, i, 0)),
        scratch_shapes=scratch,
        compiler_params=_cparams(("arbitrary", "arbitrary")),
        name="mla_attn",
    )(*args)


def _win_kernel(*refs, band, l_x):
    it = iter(refs)
    sink_ref, q_ref = next(it), next(it)
    if band:
        kx_ref, vx_ref = next(it), next(it)
    kc_ref, vc_ref, o_ref = next(it), next(it), next(it)
    w = WINDOW
    d = WIN_DH
    g = WIN_Q_HEADS // WIN_KV_HEADS
    scale = d ** -0.5
    n = pl.program_id(1)
    q = q_ref[0]
    rows = g * w
    row_id = lax.broadcasted_iota(jnp.int32, (rows, 1), 0)
    if band:
        start = pl.multiple_of(jnp.clip((n - 1) * w, 0, l_x - 3 * w), w)
        qi = n * w + lax.broadcasted_iota(jnp.int32, (rows, 3 * w), 0) % w
        kj = start + lax.broadcasted_iota(jnp.int32, (rows, 3 * w), 1)
        keep = jnp.abs(kj - qi) <= WINDOW
        kb_all = kx_ref[0, pl.ds(start, 3 * w), :]
        vb_all = vx_ref[0, pl.ds(start, 3 * w), :]
    kc_all, vc_all = kc_ref[0], vc_ref[0]
    for j in range(WIN_KV_HEADS):
        q2 = jnp.concatenate([q[:, (g * j + t) * d:(g * j + t + 1) * d] for t in range(g)], axis=0)
        sk = jnp.zeros((rows, 1), F32)
        for t in range(g):
            sk = jnp.where(row_id // w == t, sink_ref[g * j + t], sk)
        s_ctx = _dot_nt(q2, kc_all[:, j * d:(j + 1) * d]) * scale
        m = jnp.maximum(jnp.max(s_ctx, axis=-1, keepdims=True), sk)
        if band:
            s_loc = _dot_nt(q2, kb_all[:, j * d:(j + 1) * d]) * scale
            s_loc = jnp.where(keep, s_loc, NEG_INF)
            m = jnp.maximum(m, jnp.max(s_loc, axis=-1, keepdims=True))
        p_ctx = jnp.exp(s_ctx - m)
        l = jnp.sum(p_ctx, axis=-1, keepdims=True) + jnp.exp(sk - m)
        o = _dot(p_ctx.astype(BF16), vc_all[:, j * d:(j + 1) * d])
        if band:
            p_loc = jnp.exp(s_loc - m)
            l = l + jnp.sum(p_loc, axis=-1, keepdims=True)
            o = o + _dot(p_loc.astype(BF16), vb_all[:, j * d:(j + 1) * d])
        o = (o * (1.0 / l)).astype(BF16)
        for t in range(g):
            o_ref[0, :, (g * j + t) * d:(g * j + t + 1) * d] = o[t * w:(t + 1) * w, :]


def _window(win_q, win_x, win_c, sink):
    b, l, _ = win_q.shape
    lc = win_c.shape[1]
    band = win_x is not None
    kvw = WIN_KV_HEADS * WIN_DH
    args = [sink, win_q]
    specs = [pl.BlockSpec(memory_space=pltpu.SMEM),
             pl.BlockSpec((1, WINDOW, 256), lambda bi, n: (bi, n, 0))]
    if band:
        args += [win_x, win_x]
        specs += [pl.BlockSpec((1, l, kvw), lambda bi, n: (bi, 0, 2)),
                  pl.BlockSpec((1, l, kvw), lambda bi, n: (bi, 0, 3))]
    args += [win_c, win_c]
    specs += [pl.BlockSpec((1, lc, kvw), lambda bi, n: (bi, 0, 2)),
              pl.BlockSpec((1, lc, kvw), lambda bi, n: (bi, 0, 3))]
    return pl.pallas_call(
        functools.partial(_win_kernel, band=band, l_x=l),
        out_shape=jax.ShapeDtypeStruct((b, l, WIN_Q_HEADS * WIN_DH), BF16),
        grid=(b, l // WINDOW), in_specs=specs,
        out_specs=pl.BlockSpec((1, WINDOW, WIN_Q_HEADS * WIN_DH), lambda bi, n: (bi, n, 0)),
        compiler_params=_cparams(("arbitrary", "arbitrary")),
        name="window_attn",
    )(*args)


def _outproj_kernel(x_ref, yf_ref, yb_ref, rg_ref, mla_ref, win_ref, g2_ref, sh_ref, sc_ref,
                    n2_ref, wo_ref, rw_ref, xo_ref, h2_ref, afft_ref):
    tm = x_ref.shape[1]
    sub = 256 if tm % 256 == 0 else tm
    for r0 in range(0, tm, sub):
        rows = slice(r0, r0 + sub)
        y = yf_ref[0, rows, :] + yb_ref[0, rows, :]
        parts = []
        for h in range(RET_HEADS):
            yh = y[:, h * RET_DK:(h + 1) * RET_DK]
            mu = jnp.mean(yh, axis=-1, keepdims=True)
            var = jnp.mean(jnp.square(yh - mu), axis=-1, keepdims=True)
            parts.append((yh - mu) * lax.rsqrt(var + GN_EPS))
        ret = jnp.concatenate(parts, axis=1) * _silu(rg_ref[0, rows, :])
        proj = (_dot(ret.astype(BF16), wo_ref[0:256, :]) + _dot(mla_ref[0, rows, :], wo_ref[256:768, :])
                + _dot(win_ref[0, rows, :], wo_ref[768:1024, :]))
        x = x_ref[0, rows, :] + g2_ref[0] * proj
        xo_ref[0, rows, :] = x
        h2 = _rms(x, n2_ref[...]) * (1.0 + sc_ref[0]) + sh_ref[0]
        h2_ref[0, rows, :] = h2
        hi = h2.astype(BF16)
        lo = (h2 - hi.astype(F32)).astype(BF16)
        pp = _dot(jnp.concatenate([hi, lo], axis=0), rw_ref[...])
        logits = (pp[:sub, :LANES] + pp[:sub, LANES:]) + (pp[sub:, :LANES] + pp[sub:, LANES:])
        lt = logits.T[0:N_EXPERTS, :]
        et = jnp.exp(lt - jnp.max(lt, axis=0, keepdims=True))
        afft_ref[0, :, rows] = et / jnp.sum(et, axis=0, keepdims=True)


def _outproj(x, yf, yb, ret, mla, win, mod, mrow, layer, n2, w_out, rw2, tm):
    b, l, d = x.shape

    def mspec(k):
        return pl.BlockSpec((1, 1, d), lambda bi, i: ((layer * MOD_ROWS + mrow(bi)) * 6 + k, 0, 0))

    tok = lambda w: pl.BlockSpec((1, tm, w), lambda bi, i: (bi, i, 0))
    const = lambda a: pl.BlockSpec(a.shape, lambda bi, i: (0,) * a.ndim)
    return pl.pallas_call(
        _outproj_kernel,
        out_shape=[jax.ShapeDtypeStruct((b, l, d), F32), jax.ShapeDtypeStruct((b, l, d), F32),
                   jax.ShapeDtypeStruct((b, N_EXPERTS, l), F32)],
        grid=(b, l // tm),
        in_specs=[tok(d), tok(256), tok(256), pl.BlockSpec((1, tm, 256), lambda bi, i: (bi, i, 3)),
                  tok(512), tok(256), mspec(2), mspec(3), mspec(4), const(n2), const(w_out),
                  const(rw2)],
        out_specs=[tok(d), tok(d), pl.BlockSpec((1, N_EXPERTS, tm), lambda bi, i: (bi, 0, i))],
        compiler_params=_cparams(("arbitrary", "arbitrary")),
        name="outproj",
    )(x, yf, yb, ret, mla, win, mod, mod, mod, n2, w_out, rw2)


def _route_kernel(afft_ref, tri_ref, tt_ref, idx_ref, gate_ref, cum_ref, *, cap):
    a = afft_ref[0]
    ne, l = a.shape
    as_f32 = lambda bits: lax.bitcast_convert_type(bits, F32)

    def search(_, c):
        lo, hi = c
        mid = lo + ((hi - lo) >> 1)
        cnt = jnp.sum(jnp.where(a >= as_f32(mid), 1.0, 0.0), axis=1, keepdims=True)
        ok = cnt >= cap
        return jnp.where(ok, mid, lo), jnp.where(ok, hi, mid)

    lo, hi = lax.fori_loop(0, 31, search, (jnp.zeros((ne, 1), jnp.int32),
                                           jnp.full((ne, 1), 0x7F800000, jnp.int32)))
    ge = jnp.where(a >= as_f32(lo), 1.0, 0.0)
    gt = jnp.where(a >= as_f32(hi), 1.0, 0.0)
    eq = ge - gt
    need = cap - jnp.sum(gt, axis=1, keepdims=True)

    def cumsum_to_scratch(mask):
        carry = jnp.zeros((ne, 1), F32)
        for j in range(l // LANES):
            cj = _dot(mask[:, j * LANES:(j + 1) * LANES].astype(BF16), tri_ref[...]) + carry
            cum_ref[:, j * LANES:(j + 1) * LANES] = cj
            carry = cj[:, LANES - 1:LANES]

    cumsum_to_scratch(eq)
    sel = gt + eq * jnp.where(cum_ref[...] - eq < need, 1.0, 0.0)
    cumsum_to_scratch(sel)
    cum_ref[...] = sel * cum_ref[...]
    slot = (lax.broadcasted_iota(jnp.int32, (cap, l), 0) + 1).astype(F32)

    def per_expert(e, _):
        onehot = jnp.where(cum_ref[pl.ds(e, 1), :] == slot, 1.0, 0.0).astype(BF16)
        a_e = afft_ref[0, pl.ds(e, 1), :]
        a1 = a_e.astype(BF16).astype(F32)
        a2 = (a_e - a1).astype(BF16).astype(F32)
        a3 = a_e - a1 - a2
        lhs = jnp.concatenate([tt_ref[0:2, :], a1, a2, a3, jnp.zeros((3, l), F32)], axis=0).astype(BF16)
        r = _dot_nt(lhs, onehot)
        idx_ref[0, pl.ds(e, 1), :, :] = (r[0:1, :] * LANES + r[1:2, :]).astype(jnp.int32).reshape(1, 1, cap)
        gate_ref[0, pl.ds(e, 1), :, :] = (r[2:3, :] + r[3:4, :] + r[4:5, :]).reshape(1, 1, cap)
        return 0

    lax.fori_loop(0, ne, per_expert, 0)


def _route(afft, cap):
    b, ne, l = afft.shape
    u = np.arange(LANES)
    tri = jnp.asarray(u[:, None] <= u[None, :], BF16)
    t = np.arange(l)
    tt = np.zeros((8, l), np.float32)
    tt[0], tt[1] = t // LANES, t % LANES
    tt = jnp.asarray(tt, F32)
    out_spec = pl.BlockSpec((1, ne, 1, cap), lambda bi: (bi, 0, 0, 0))
    return pl.pallas_call(
        functools.partial(_route_kernel, cap=cap),
        out_shape=[jax.ShapeDtypeStruct((b, ne, 1, cap), jnp.int32), jax.ShapeDtypeStruct((b, ne, 1, cap), F32)],
        grid=(b,),
        in_specs=[pl.BlockSpec((1, ne, l), lambda bi: (bi, 0, 0)),
                  pl.BlockSpec(tri.shape, lambda bi: (0, 0)), pl.BlockSpec(tt.shape, lambda bi: (0, 0))],
        out_specs=[out_spec, out_spec],
        scratch_shapes=[pltpu.VMEM((ne, l), F32)],
        compiler_params=_cparams(("arbitrary",)),
        name="route",
    )(afft, tri, tt)


def _ffn_kernel(idx_ref, h_ref, wg_ref, wu_ref, wd_ref, ys_ref, xa_ref, xb_ref, *, nb, cap, ne):
    e = pl.program_id(1)

    def gather_rolled(ex, dst):
        for s in range(nb):
            def body(j, _):
                dst[pl.ds(s * cap + j, 1), :] = h_ref[s, pl.ds(idx_ref[s, ex, 0, j], 1), :]
                return 0
            lax.fori_loop(0, cap, body, 0, unroll=8)

    def gather_unrolled(ex, dst):
        for s in range(nb):
            for j in range(cap):
                dst[s * cap + j:s * cap + j + 1, :] = h_ref[s, pl.ds(idx_ref[s, ex, 0, j], 1), :]

    def ffn(src):
        xb = src[...].astype(BF16)
        hid = _silu(_dot(xb, wg_ref[0])) * _dot(xb, wu_ref[0])
        y = _dot(hid.astype(BF16), wd_ref[0])
        for s in range(nb):
            ys_ref[s, 0] = y[s * cap:(s + 1) * cap, :]

    @pl.when(e == 0)
    def _():
        gather_rolled(0, xa_ref)

    e_next = jnp.minimum(e + 1, ne - 1)

    @pl.when(e % 2 == 0)
    def _():
        gather_unrolled(e_next, xb_ref)
        ffn(xa_ref)

    @pl.when(e % 2 == 1)
    def _():
        gather_unrolled(e_next, xa_ref)
        ffn(xb_ref)


def _expert_ffn(idx, h2, wg, wu, wd, nb):
    b, l, d = h2.shape
    ne, cap = idx.shape[1], idx.shape[3]
    ff = wg.shape[2]
    assert ne % 2 == 0
    return pl.pallas_call(
        functools.partial(_ffn_kernel, nb=nb, cap=cap, ne=ne),
        out_shape=jax.ShapeDtypeStruct((b, ne, cap, d), F32),
        grid=(b // nb, ne),
        in_specs=[pl.BlockSpec((nb, ne, 1, cap), lambda bi, e: (bi, 0, 0, 0), memory_space=pltpu.SMEM),
                  pl.BlockSpec((nb, l, d), lambda bi, e: (bi, 0, 0), pipeline_mode=pl.Buffered(1)),
                  pl.BlockSpec((1, d, ff), lambda bi, e: (e, 0, 0)),
                  pl.BlockSpec((1, d, ff), lambda bi, e: (e, 0, 0)),
                  pl.BlockSpec((1, ff, d), lambda bi, e: (e, 0, 0))],
        out_specs=pl.BlockSpec((nb, 1, cap, d), lambda bi, e: (bi, e, 0, 0)),
        scratch_shapes=[pltpu.VMEM((nb * cap, d), F32), pltpu.VMEM((nb * cap, d), F32)],
        compiler_params=_cparams(("arbitrary", "arbitrary")),
        name="expert_ffn",
    )(idx, h2, wg, wu, wd)


def _scatter_kernel(idx_ref, gate_ref, ys_ref, acc_ref, *, nb, cap):
    @pl.when(pl.program_id(1) == 0)
    def _():
        acc_ref[...] = jnp.zeros_like(acc_ref)

    for s in range(nb):
        def add(j, _):
            t = idx_ref[s, 0, 0, j]
            acc_ref[s, pl.ds(t, 1), :] = (acc_ref[s, pl.ds(t, 1), :]
                                          + ys_ref[s, 0, pl.ds(j, 1), :] * gate_ref[s, 0, 0, j])
            return 0
        lax.fori_loop(0, cap, add, 0, unroll=8)


def _scatter(idx, gate, ys, l, nb):
    b, ne, cap, d = ys.shape
    smem = lambda: pl.BlockSpec((nb, 1, 1, cap), lambda bi, e: (bi, e, 0, 0), memory_space=pltpu.SMEM)
    return pl.pallas_call(
        functools.partial(_scatter_kernel, nb=nb, cap=cap),
        out_shape=jax.ShapeDtypeStruct((b, l, d), F32),
        grid=(b // nb, ne),
        in_specs=[smem(), smem(), pl.BlockSpec((nb, 1, cap, d), lambda bi, e: (bi, e, 0, 0))],
        out_specs=pl.BlockSpec((nb, l, d), lambda bi, e: (bi, 0, 0)),
        compiler_params=_cparams(("arbitrary", "arbitrary")),
        name="moe_scatter",
    )(idx, gate, ys)


def _final_kernel(x_ref, acc_ref, g5_ref, fg_ref, o_ref):
    o_ref[0] = _rms(x_ref[0] + g5_ref[0] * acc_ref[0], fg_ref[...])


def _final(x, acc, mod, layer, fg, tm):
    b, l, d = x.shape
    tok = pl.BlockSpec((1, tm, d), lambda bi, i: (bi, i, 0))
    return pl.pallas_call(
        _final_kernel,
        out_shape=jax.ShapeDtypeStruct((b, l, d), F32),
        grid=(b, l // tm),
        in_specs=[tok, tok, pl.BlockSpec((1, 1, d), lambda bi, i: ((layer * MOD_ROWS + bi) * 6 + 5, 0, 0)),
                  pl.BlockSpec(fg.shape, lambda bi, i: (0, 0))],
        out_specs=tok,
        compiler_params=_cparams(("arbitrary", "arbitrary")),
        name="final_norm",
    )(x, acc, mod, fg)


def _partner(dh):
    q = dh // 4
    return np.concatenate([np.arange(q, 2 * q), np.arange(0, q), np.arange(3 * q, 4 * q), np.arange(2 * q, 3 * q)])


def _rope_tables(n_tok, dh, reps):
    rows = n_tok // GRID_W
    row = jnp.repeat(jnp.arange(rows, dtype=jnp.int32), GRID_W)
    col = jnp.tile(jnp.arange(GRID_W, dtype=jnp.int32), rows)
    half = dh // 2
    inv = ROPE_BASE ** (-jnp.arange(0, half, 2, dtype=jnp.float32) / half)
    ar = row.astype(jnp.float32)[:, None] * inv[None, :]
    ac = col.astype(jnp.float32)[:, None] * inv[None, :]
    cos = jnp.concatenate([jnp.cos(ar), jnp.cos(ar), jnp.cos(ac), jnp.cos(ac)], axis=1)
    sin = jnp.concatenate([-jnp.sin(ar), jnp.sin(ar), -jnp.sin(ac), jnp.sin(ac)], axis=1)
    return jnp.tile(cos, (1, reps)), jnp.tile(sin, (1, reps))


def _extend_w_in(w):
    d = w.shape[0]
    wq, wk, wv = w[:, 1440:1696], w[:, 1696:1824], w[:, 1824:1952]
    kr = w[:, 1408:1440]
    pq = np.concatenate([h * WIN_DH + _partner(WIN_DH) for h in range(WIN_Q_HEADS)])
    pk = np.concatenate([h * WIN_DH + _partner(WIN_DH) for h in range(WIN_KV_HEADS)])
    ext = jnp.concatenate([w[:, 0:1408], wq, wq[:, pq], wk, wk[:, pk], wv, kr, kr[:, _partner(MLA_ROPE)],
                           jnp.zeros((d, N_EXT - C_KR - 2 * MLA_ROPE), w.dtype)], axis=1)
    return ext.astype(BF16)


def kernel(x, c, ctx, c_ctx, norm1_g, norm2_g, ada_w, ada_b, w_in, ret_decay_f, ret_decay_b,
           mla_qnorm_g, mla_kvnorm_g, mla_w_uq, mla_w_uk, mla_w_uv, win_sink, w_out,
           router_w, exp_w_gate, exp_w_up, exp_w_down, final_g):
    b, l, d = x.shape
    lc = ctx.shape[1]
    depth = w_in.shape[0]
    assert b + 1 <= MOD_ROWS and l % 512 == 0 and lc % RET_CHUNK == 0 and l >= 3 * WINDOW

    c_all = jnp.concatenate([c, c_ctx[None, :], jnp.zeros((MOD_ROWS - b - 1, d), F32)], axis=0)
    mod = _modulation(c_all, ada_w, ada_b).reshape(depth * MOD_ROWS * 6, 1, d)
    row_x = lambda bi: bi
    row_c = lambda bi: b

    wabs = _absorbed_q_weights(mla_w_uq, mla_w_uk)
    cw, sw = _rope_tables(l, WIN_DH, LANES // WIN_DH)
    cm, sm = _rope_tables(l, MLA_ROPE, 1)
    cq, sq = _rope_tables(l, MLA_ROPE, LANES // MLA_ROPE)
    pr = np.concatenate([h * MLA_ROPE + _partner(MLA_ROPE) for h in range(MLA_HEADS)])

    tm = 512
    tq = 128
    tk = 1024 if l % 2048 == 0 else l // 2
    acc_x = acc_c = None
    for layer in range(depth):
        need_ctx = layer < depth - 1
        w_ext = _extend_w_in(w_in[layer])
        g1 = norm1_g[layer].reshape(1, d)
        qng = mla_qnorm_g[layer].reshape(1, -1)
        kvg = mla_kvnorm_g[layer].reshape(1, -1)
        uq_rope = mla_w_uq[layer][:, :, MLA_NOPE:].reshape(MLA_Q_RANK, MLA_HEADS * MLA_ROPE)
        wq_all = jnp.concatenate([wabs[layer], uq_rope, uq_rope[:, pr]], axis=1).astype(BF16)
        wuv = jnp.transpose(mla_w_uv[layer], (1, 0, 2)).astype(BF16)
        dec = jnp.broadcast_to(jnp.concatenate([ret_decay_f[layer], ret_decay_b[layer]])[:, None],
                               (2 * RET_HEADS, LANES)).astype(F32)
        wo = w_out[layer].astype(BF16)
        n2 = norm2_g[layer].reshape(1, d)
        rw = jnp.pad(router_w[layer], ((0, 0), (0, LANES - N_EXPERTS)))
        rwh = rw.astype(BF16)
        rw2 = jnp.concatenate([rwh, (rw - rwh.astype(F32)).astype(BF16)], axis=1)
        wg, wu, wd = (exp_w_gate[layer].astype(BF16), exp_w_up[layer].astype(BF16),
                      exp_w_down[layer].astype(BF16))

        res_x = None if layer == 0 else (acc_x, layer - 1)
        res_c = None if layer == 0 else (acc_c, layer - 1)
        x, ret_x, cqn_x, kcat_x, win_x = _inproj(x, mod, row_x, layer, g1, w_ext, qng, kvg,
                                                 (cw, sw, cm, sm), res_x, tm)
        ctx, ret_c, cqn_c, kcat_c, win_c = _inproj(ctx, mod, row_c, layer, g1, w_ext, qng, kvg,
                                                   None, res_c, min(tm, lc))
        ycf, ycb, yxf, yxb = _retention(ret_c, ret_x, dec, 4 if l % (4 * RET_CHUNK) == 0 else 1)
        mla_x = _mla(cqn_x, wq_all, (cq, sq), kcat_c, kcat_x, wuv, tq, tk)
        wn_x = _window(win_x, win_x, win_c, win_sink[layer])
        x, h2x, afft_x = _outproj(x, yxf, yxb, ret_x, mla_x, wn_x, mod, row_x, layer, n2, wo, rw2, tm)
        idx_x, gate_x = _route(afft_x, CAPACITY_FACTOR * l // N_EXPERTS)
        acc_x = _scatter(idx_x, gate_x, _expert_ffn(idx_x, h2x, wg, wu, wd, 1), l, 1)
        if need_ctx:
            mla_c = _mla(cqn_c, wq_all, None, kcat_c, None, wuv, min(tq, lc), tk)
            wn_c = _window(win_c, None, win_c, win_sink[layer])
            ctx, h2c, afft_c = _outproj(ctx, ycf, ycb, ret_c, mla_c, wn_c, mod, row_c, layer, n2, wo,
                                        rw2, min(tm, lc))
            idx_c, gate_c = _route(afft_c, CAPACITY_FACTOR * lc // N_EXPERTS)
            acc_c = _scatter(idx_c, gate_c, _expert_ffn(idx_c, h2c, wg, wu, wd, b), lc, b)
    return _final(x, acc_x, mod, depth - 1, final_g.reshape(1, d), tm)
```

```python
import functools

import numpy as np
import jax
import jax.numpy as jnp
from jax import lax
from jax.experimental import pallas as pl
from jax.experimental.pallas import tpu as pltpu

F32 = jnp.float32
BF16 = jnp.bfloat16

D_MODEL = 1024
DEPTH = 2
GRID_W = 64
RET_HEADS = 4
RET_DK = 64
RET_CHUNK = 128
MLA_HEADS = 8
MLA_Q_RANK = 256
MLA_KV_RANK = 128
MLA_NOPE = 64
MLA_ROPE = 32
MLA_DV = 64
WIN_Q_HEADS = 4
WIN_KV_HEADS = 2
WIN_DH = 64
WINDOW = 128
N_EXPERTS = 16
EXPERT_FF = 768
CAPACITY_FACTOR = 2
ROPE_BASE = 10000.0
NORM_EPS = 1e-6
GN_EPS = 1e-5
NEG_INF = -1e30
LOG2E = 1.4426950408889634

LANES = 128
MOD_ROWS = 16
KCAT = MLA_KV_RANK + MLA_ROPE
VMEM_LIMIT = 56 * 1024 * 1024

C_RET = 0
C_CQ = 1024
C_CKV = 1280
C_WQ = 1408
C_WQP = 1664
C_WK = 1920
C_WKP = 2048
C_WV = 2176
C_KR = 2304
N_EXT = 2432


def _cparams(sem):
    return pltpu.CompilerParams(dimension_semantics=sem, vmem_limit_bytes=VMEM_LIMIT)


def _dot(a, b):
    return jnp.dot(a, b, preferred_element_type=F32)


def _dot_nt(a, b):
    return lax.dot_general(a, b, (((1,), (1,)), ((), ())), preferred_element_type=F32)


def _dot_tn(a, b):
    return lax.dot_general(a, b, (((0,), (0,)), ((), ())), preferred_element_type=F32)


def _rms(x, g):
    return x * lax.rsqrt(jnp.mean(x * x, axis=-1, keepdims=True) + NORM_EPS) * g


def _silu(x):
    return x * jax.nn.sigmoid(x)


def _mod_kernel(c_ref, w_ref, b_ref, o_ref):
    o_ref[0] = jnp.dot(_silu(c_ref[...]), w_ref[0], preferred_element_type=F32,
                       precision=lax.Precision.HIGHEST) + b_ref[0]


def _modulation(c_all, ada_w, ada_b):
    depth, d, n = ada_w.shape
    tn = 1024
    return pl.pallas_call(
        _mod_kernel,
        out_shape=jax.ShapeDtypeStruct((depth, MOD_ROWS, n), F32),
        grid=(depth, n // tn),
        in_specs=[pl.BlockSpec((MOD_ROWS, d), lambda l, j: (0, 0)),
                  pl.BlockSpec((1, d, tn), lambda l, j: (l, 0, j)),
                  pl.BlockSpec((1, 1, tn), lambda l, j: (l, 0, j))],
        out_specs=pl.BlockSpec((1, MOD_ROWS, tn), lambda l, j: (l, 0, j)),
        compiler_params=_cparams(("arbitrary", "arbitrary")),
        name="adaln_mod",
    )(c_all, ada_w, ada_b.reshape(depth, 1, n))


def _wabs_kernel(uq_ref, uk_ref, o_ref):
    o_ref[0] = jnp.dot(uq_ref[0, 0], uk_ref[0, 0], preferred_element_type=F32,
                       precision=lax.Precision.HIGHEST)


def _absorbed_q_weights(w_uq, w_uk):
    depth = w_uq.shape[0]
    uq = jnp.transpose(w_uq[..., :MLA_NOPE], (0, 2, 1, 3))
    uk = jnp.transpose(w_uk, (0, 2, 3, 1))
    return pl.pallas_call(
        _wabs_kernel,
        out_shape=jax.ShapeDtypeStruct((depth, MLA_Q_RANK, MLA_HEADS * MLA_KV_RANK), F32),
        grid=(depth, MLA_HEADS),
        in_specs=[pl.BlockSpec((1, 1, MLA_Q_RANK, MLA_NOPE), lambda l, h: (l, h, 0, 0)),
                  pl.BlockSpec((1, 1, MLA_NOPE, MLA_KV_RANK), lambda l, h: (l, h, 0, 0))],
        out_specs=pl.BlockSpec((1, MLA_Q_RANK, MLA_KV_RANK), lambda l, h: (l, 0, h)),
        compiler_params=_cparams(("arbitrary", "arbitrary")),
        name="mla_absorb",
    )(uq, uk)


def _inproj_kernel(*refs, rope, resid):
    it = iter(refs)
    x_ref = next(it)
    if resid:
        acc_ref, g5_ref = next(it), next(it)
    g1_ref, sh_ref, sc_ref, w_ref, qng_ref, kvg_ref = (next(it) for _ in range(6))
    if rope:
        cw_ref, sw_ref, cm_ref, sm_ref = (next(it) for _ in range(4))
    if resid:
        xo_ref = next(it)
    ret_ref, cqn_ref, kcat_ref, win_ref = (next(it) for _ in range(4))

    x = x_ref[0]
    if resid:
        x = x + g5_ref[0] * acc_ref[0]
        xo_ref[0] = x
    h = _rms(x, g1_ref[...]) * (1.0 + sc_ref[0]) + sh_ref[0]
    p = _dot(h.astype(BF16), w_ref[...])

    ret_ref[0] = p[:, C_RET:C_CQ]
    cqn_ref[0] = _rms(p[:, C_CQ:C_CKV], qng_ref[...]).astype(BF16)
    kvn = _rms(p[:, C_CKV:C_WQ], kvg_ref[...])
    wq, wk, wv = p[:, C_WQ:C_WQP], p[:, C_WK:C_WKP], p[:, C_WV:C_KR]
    kr = p[:, C_KR:C_KR + MLA_ROPE]
    if rope:
        wqp, wkp = p[:, C_WQP:C_WK], p[:, C_WKP:C_WV]
        krp = p[:, C_KR + MLA_ROPE:C_KR + 2 * MLA_ROPE]
        cw, sw = cw_ref[...], sw_ref[...]
        wq = jnp.concatenate([wq[:, :LANES] * cw + wqp[:, :LANES] * sw,
                              wq[:, LANES:] * cw + wqp[:, LANES:] * sw], axis=1)
        wk = wk * cw + wkp * sw
        kr = kr * cm_ref[...] + krp * sm_ref[...]
    kcat_ref[0, :, 0:MLA_KV_RANK] = kvn.astype(BF16)
    kcat_ref[0, :, MLA_KV_RANK:KCAT] = kr.astype(BF16)
    win_ref[0, :, 0:256] = wq.astype(BF16)
    win_ref[0, :, 256:384] = wk.astype(BF16)
    win_ref[0, :, 384:512] = wv.astype(BF16)


def _inproj(x, mod, mrow, layer, g1, w_ext, qng, kvg, tables, resid, tm):
    b, l, d = x.shape
    rope = tables is not None

    def mspec(k, lay):
        return pl.BlockSpec((1, 1, d), lambda bi, i: ((lay * MOD_ROWS + mrow(bi)) * 6 + k, 0, 0))

    tok = lambda w: pl.BlockSpec((1, tm, w), lambda bi, i: (bi, i, 0))
    const = lambda a: pl.BlockSpec(a.shape, lambda bi, i: (0,) * a.ndim)
    args, specs = [x], [tok(d)]
    if resid is not None:
        args += [resid[0], mod]
        specs += [tok(d), mspec(5, resid[1])]
    args += [g1, mod, mod, w_ext, qng, kvg]
    specs += [const(g1), mspec(0, layer), mspec(1, layer), const(w_ext), const(qng), const(kvg)]
    if rope:
        for t in tables:
            args.append(t)
            specs.append(pl.BlockSpec((tm, t.shape[1]), lambda bi, i: (i, 0)))
    out_shape, out_specs = [], []
    if resid is not None:
        out_shape.append(jax.ShapeDtypeStruct((b, l, d), F32))
        out_specs.append(tok(d))
    out_shape += [jax.ShapeDtypeStruct((b, l, 1024), F32), jax.ShapeDtypeStruct((b, l, MLA_Q_RANK), BF16),
                  jax.ShapeDtypeStruct((b, l, KCAT), BF16), jax.ShapeDtypeStruct((b, l, 512), BF16)]
    out_specs += [tok(1024), tok(MLA_Q_RANK), tok(KCAT), tok(512)]
    outs = pl.pallas_call(
        functools.partial(_inproj_kernel, rope=rope, resid=resid is not None),
        out_shape=out_shape, grid=(b, l // tm), in_specs=specs, out_specs=out_specs,
        compiler_params=_cparams(("arbitrary", "arbitrary")),
        name="inproj",
    )(*args)
    if resid is None:
        return (x,) + tuple(outs)
    return tuple(outs)


def _ret_kernel(dec_ref, c_ref, xf_ref, xb_ref, ycf_ref, ycb_ref, yxf_ref, yxb_ref,
                s_ref, dm_ref, qw_ref, kw_ref, gc_ref):
    n = pl.program_id(1)
    c = RET_CHUNK
    dk = RET_DK

    @pl.when(n == 0)
    def _():
        s_ref[...] = jnp.zeros_like(s_ref)
        lg = jnp.log1p(-jnp.exp2(dec_ref[...]))
        ii = lax.broadcasted_iota(jnp.int32, (c, c), 0).astype(F32)
        jj = lax.broadcasted_iota(jnp.int32, (c, c), 1).astype(F32)
        ir = lax.broadcasted_iota(jnp.int32, (c, dk), 0).astype(F32)
        for h in range(RET_HEADS):
            lf = lg[h:h + 1, :]
            lb = lg[RET_HEADS + h:RET_HEADS + h + 1, :]
            keep_f = ii >= jj
            dm_ref[h] = jnp.where(keep_f, jnp.exp(lf * jnp.where(keep_f, ii - jj, 0.0)), 0.0)
            keep_b = jj > ii
            dm_ref[RET_HEADS + h] = jnp.where(keep_b, jnp.exp(lb * jnp.where(keep_b, jj - ii, 0.0)), 0.0)
            qw_ref[h] = jnp.exp(lf[:, :dk] * (ir + 1.0))
            kw_ref[h] = jnp.exp(lf[:, :dk] * (c - 1.0 - ir))
            qw_ref[RET_HEADS + h] = jnp.exp(lb[:, :dk] * (c - ir))
            kw_ref[RET_HEADS + h] = jnp.exp(lb[:, :dk] * ir)
            gc_ref[h] = jnp.exp(lf[:, :dk] * float(c))
            gc_ref[RET_HEADS + h] = jnp.exp(lb[:, :dk] * float(c))

    def run(src_ref, dst_ref, d):
        n_sub = src_ref.shape[1] // c
        order = range(n_sub) if d == 0 else range(n_sub - 1, -1, -1)
        for h in range(RET_HEADS):
            r = d * RET_HEADS + h
            intra, kv, qd = {}, {}, {}
            for g in order:
                rows = slice(g * c, (g + 1) * c)
                q = src_ref[0, rows, h * dk:(h + 1) * dk]
                k = src_ref[0, rows, 256 + h * dk:256 + (h + 1) * dk] * (RET_DK ** -0.5)
                v = src_ref[0, rows, 512 + h * dk:512 + (h + 1) * dk].astype(BF16)
                sc = _dot_nt(q.astype(BF16), k.astype(BF16)) * dm_ref[r]
                intra[g] = _dot(sc.astype(BF16), v)
                kv[g] = _dot_tn((k * kw_ref[r]).astype(BF16), v)
                qd[g] = (q * qw_ref[r]).astype(BF16)
            s = s_ref[r]
            for g in order:
                dst_ref[0, g * c:(g + 1) * c, h * dk:(h + 1) * dk] = intra[g] + _dot(qd[g], s.astype(BF16))
                s = gc_ref[r] * s + kv[g]
            s_ref[r] = s

    @pl.when(n == 0)
    def _():
        run(c_ref, ycf_ref, 0)
        run(c_ref, ycb_ref, 1)

    @pl.when(n > 0)
    def _():
        run(xf_ref, yxf_ref, 0)
        run(xb_ref, yxb_ref, 1)


def _retention(ret_c, ret_x, dec, g_sub):
    b, lc, _ = ret_c.shape
    lx = ret_x.shape[1]
    c = RET_CHUNK
    blk = g_sub * c
    assert lc % c == 0 and lx % blk == 0
    nx = lx // blk
    ci = lambda bi, n: (bi, 0, 0)
    xfi = lambda bi, n: (bi, jnp.maximum(n - 1, 0), 0)
    xbi = lambda bi, n: (bi, nx - 1 - jnp.maximum(n - 1, 0), 0)
    ydim = RET_HEADS * RET_DK
    return pl.pallas_call(
        _ret_kernel,
        out_shape=[jax.ShapeDtypeStruct((b, lc, ydim), F32), jax.ShapeDtypeStruct((b, lc, ydim), F32),
                   jax.ShapeDtypeStruct((b, lx, ydim), F32), jax.ShapeDtypeStruct((b, lx, ydim), F32)],
        grid=(b, 1 + nx),
        in_specs=[pl.BlockSpec(dec.shape, lambda bi, n: (0, 0)),
                  pl.BlockSpec((1, lc, 768), ci),
                  pl.BlockSpec((1, blk, 768), xfi), pl.BlockSpec((1, blk, 768), xbi)],
        out_specs=[pl.BlockSpec((1, lc, ydim), ci), pl.BlockSpec((1, lc, ydim), ci),
                   pl.BlockSpec((1, blk, ydim), xfi), pl.BlockSpec((1, blk, ydim), xbi)],
        scratch_shapes=[pltpu.VMEM((2 * RET_HEADS, RET_DK, RET_DK), F32),
                        pltpu.VMEM((2 * RET_HEADS, c, c), F32),
                        pltpu.VMEM((2 * RET_HEADS, c, RET_DK), F32),
                        pltpu.VMEM((2 * RET_HEADS, c, RET_DK), F32),
                        pltpu.VMEM((2 * RET_HEADS, 1, RET_DK), F32)],
        compiler_params=_cparams(("arbitrary", "arbitrary")),
        name="retention",
    )(dec, ret_c, ret_x, ret_x)


def _mla_kernel(*refs, rope, band, tq):
    it = iter(refs)
    cqn_ref, wq_ref = next(it), next(it)
    if rope:
        cq_ref, sq_ref = next(it), next(it)
    k_ref, wuv_ref, sink_ref, wnq_ref = (next(it) for _ in range(4))
    wkx_ref, wvx_ref = (next(it), next(it)) if band else (None, None)
    wkc_ref, wvc_ref, o_ref, wno_ref, qcat_ref, s_ref, m_ref = (next(it) for _ in range(7))

    nh = MLA_HEADS
    tk = k_ref.shape[1] // 2
    scale = (MLA_NOPE + MLA_ROPE) ** -0.5 * LOG2E
    qa = _dot(cqn_ref[0], wq_ref[...]) * scale
    qr = qa[:, nh * MLA_KV_RANK:nh * MLA_KV_RANK + nh * MLA_ROPE]
    if rope:
        qp = qa[:, nh * MLA_KV_RANK + nh * MLA_ROPE:]
        cq, sq = cq_ref[...], sq_ref[...]
        qr = jnp.concatenate([qr[:, :LANES] * cq + qp[:, :LANES] * sq,
                              qr[:, LANES:] * cq + qp[:, LANES:] * sq], axis=1)
    for h in range(nh):
        qcat_ref[h * tq:(h + 1) * tq, 0:MLA_KV_RANK] = qa[:, h * MLA_KV_RANK:(h + 1) * MLA_KV_RANK].astype(BF16)
        qcat_ref[h * tq:(h + 1) * tq, MLA_KV_RANK:KCAT] = qr[:, h * MLA_ROPE:(h + 1) * MLA_ROPE].astype(BF16)
    once = jnp.minimum(pl.program_id(1) + 1, 1)

    def run_once(fn):
        lax.fori_loop(0, once, lambda i, c: (fn(), c)[1], 0)

    def pass1():
        q = qcat_ref[...]
        mx = None
        for c in range(2):
            s = _dot_nt(q, k_ref[0, c * tk:(c + 1) * tk, :])
            s_ref[c] = s
            for j in range(0, tk, LANES):
                mx = s[:, j:j + LANES] if mx is None else jnp.maximum(mx, s[:, j:j + LANES])
        m_ref[...] = jnp.broadcast_to(jnp.max(mx, axis=-1, keepdims=True), m_ref.shape)
        _window_block(pl.program_id(1), sink_ref, wnq_ref, wkx_ref, wvx_ref, wkc_ref, wvc_ref, wno_ref)

    def pass2():
        m = m_ref[:, 0:1]
        ones_col = jnp.where(lax.broadcasted_iota(jnp.int32, (tk, LANES), 1) == 0, 1.0, 0.0).astype(BF16)

        def part(c):
            v_aug = jnp.concatenate([k_ref[0, c * tk:(c + 1) * tk, 0:MLA_KV_RANK], ones_col], axis=1)
            return _dot(jnp.exp2(s_ref[c] - m).astype(BF16), v_aug)

        acc = part(0) + part(1)
        o = (acc[:, 0:MLA_KV_RANK] * (1.0 / acc[:, MLA_KV_RANK:MLA_KV_RANK + 1])).astype(BF16)
        for h in range(nh):
            o_ref[0, :, h * MLA_DV:(h + 1) * MLA_DV] = _dot(o[h * tq:(h + 1) * tq, :], wuv_ref[h]).astype(BF16)

    run_once(pass1)
    run_once(pass2)


def _attention(cqn, wq_all, tables, keys, wuv, win_q, win_x, win_c, sink):
    b, l, _ = cqn.shape
    lk = keys.shape[1]
    lc = win_c.shape[1]
    tq = WINDOW
    rope = tables is not None
    band = win_x is not None
    kvw = WIN_KV_HEADS * WIN_DH
    assert lk % (2 * LANES) == 0 and l % tq == 0
    args = [cqn, wq_all]
    specs = [pl.BlockSpec((1, tq, MLA_Q_RANK), lambda bi, i: (bi, i, 0)),
             pl.BlockSpec(wq_all.shape, lambda bi, i: (0, 0))]
    if rope:
        for t in tables:
            args.append(t)
            specs.append(pl.BlockSpec((tq, LANES), lambda bi, i: (i, 0)))
    args.append(keys)
    specs.append(pl.BlockSpec((1, lk, KCAT), lambda bi, i: (bi, 0, 0)))
    args += [wuv, sink, win_q]
    specs += [pl.BlockSpec(wuv.shape, lambda bi, i: (0, 0, 0)), pl.BlockSpec(memory_space=pltpu.SMEM),
              pl.BlockSpec((1, tq, 256), lambda bi, i: (bi, i, 0))]
    if band:
        args += [win_x, win_x]
        specs += [pl.BlockSpec((1, l, kvw), lambda bi, i: (bi, 0, 2)),
                  pl.BlockSpec((1, l, kvw), lambda bi, i: (bi, 0, 3))]
    args += [win_c, win_c]
    specs += [pl.BlockSpec((1, lc, kvw), lambda bi, i: (bi, 0, 2)),
              pl.BlockSpec((1, lc, kvw), lambda bi, i: (bi, 0, 3))]
    rows = MLA_HEADS * tq
    scratch = [pltpu.VMEM((rows, KCAT), BF16), pltpu.VMEM((2, rows, lk // 2), F32),
               pltpu.VMEM((rows, LANES), F32)]
    return pl.pallas_call(
        functools.partial(_mla_kernel, rope=rope, band=band, tq=tq),
        out_shape=[jax.ShapeDtypeStruct((b, l, MLA_HEADS * MLA_DV), BF16),
                   jax.ShapeDtypeStruct((b, l, WIN_Q_HEADS * WIN_DH), BF16)],
        grid=(b, l // tq), in_specs=specs,
        out_specs=[pl.BlockSpec((1, tq, MLA_HEADS * MLA_DV), lambda bi, i: (bi, i, 0)),
                   pl.BlockSpec((1, tq, WIN_Q_HEADS * WIN_DH), lambda bi, i: (bi, i, 0))],
        scratch_shapes=scratch,
        compiler_params=_cparams(("arbitrary", "arbitrary")),
        name="attention",
    )(*args)


def _window_block(n, sink_ref, q_ref, kx_ref, vx_ref, kc_ref, vc_ref, o_ref):
    band = kx_ref is not None
    w = WINDOW
    d = WIN_DH
    g = WIN_Q_HEADS // WIN_KV_HEADS
    scale = d ** -0.5
    q = q_ref[0]
    rows = g * w
    row_id = lax.broadcasted_iota(jnp.int32, (rows, 1), 0)
    if band:
        l_x = kx_ref.shape[1]
        start = pl.multiple_of(jnp.clip((n - 1) * w, 0, l_x - 3 * w), w)
        qi = n * w + lax.broadcasted_iota(jnp.int32, (rows, 3 * w), 0) % w
        kj = start + lax.broadcasted_iota(jnp.int32, (rows, 3 * w), 1)
        keep = jnp.abs(kj - qi) <= WINDOW
        kb_all = kx_ref[0, pl.ds(start, 3 * w), :]
        vb_all = vx_ref[0, pl.ds(start, 3 * w), :]
    kc_all, vc_all = kc_ref[0], vc_ref[0]
    for j in range(WIN_KV_HEADS):
        q2 = jnp.concatenate([q[:, (g * j + t) * d:(g * j + t + 1) * d] for t in range(g)], axis=0)
        sk = jnp.zeros((rows, 1), F32)
        for t in range(g):
            sk = jnp.where(row_id // w == t, sink_ref[g * j + t], sk)
        s_ctx = _dot_nt(q2, kc_all[:, j * d:(j + 1) * d]) * scale
        m = jnp.maximum(jnp.max(s_ctx, axis=-1, keepdims=True), sk)
        if band:
            s_loc = _dot_nt(q2, kb_all[:, j * d:(j + 1) * d]) * scale
            s_loc = jnp.where(keep, s_loc, NEG_INF)
            m = jnp.maximum(m, jnp.max(s_loc, axis=-1, keepdims=True))
        p_ctx = jnp.exp(s_ctx - m)
        l = jnp.sum(p_ctx, axis=-1, keepdims=True) + jnp.exp(sk - m)
        o = _dot(p_ctx.astype(BF16), vc_all[:, j * d:(j + 1) * d])
        if band:
            p_loc = jnp.exp(s_loc - m)
            l = l + jnp.sum(p_loc, axis=-1, keepdims=True)
            o = o + _dot(p_loc.astype(BF16), vb_all[:, j * d:(j + 1) * d])
        o = (o * (1.0 / l)).astype(BF16)
        for t in range(g):
            o_ref[0, :, (g * j + t) * d:(g * j + t + 1) * d] = o[t * w:(t + 1) * w, :]


def _outproj_kernel(x_ref, yf_ref, yb_ref, rg_ref, mla_ref, win_ref, g2_ref, sh_ref, sc_ref,
                    n2_ref, wo_ref, rw_ref, xo_ref, h2_ref, afft_ref):
    tm = x_ref.shape[1]
    sub = 256 if tm % 256 == 0 else tm
    for r0 in range(0, tm, sub):
        rows = slice(r0, r0 + sub)
        y = yf_ref[0, rows, :] + yb_ref[0, rows, :]
        parts = []
        for h in range(RET_HEADS):
            yh = y[:, h * RET_DK:(h + 1) * RET_DK]
            mu = jnp.mean(yh, axis=-1, keepdims=True)
            var = jnp.mean(jnp.square(yh - mu), axis=-1, keepdims=True)
            parts.append((yh - mu) * lax.rsqrt(var + GN_EPS))
        ret = jnp.concatenate(parts, axis=1) * _silu(rg_ref[0, rows, :])
        proj = (_dot(ret.astype(BF16), wo_ref[0:256, :]) + _dot(mla_ref[0, rows, :], wo_ref[256:768, :])
                + _dot(win_ref[0, rows, :], wo_ref[768:1024, :]))
        x = x_ref[0, rows, :] + g2_ref[0] * proj
        xo_ref[0, rows, :] = x
        h2 = _rms(x, n2_ref[...]) * (1.0 + sc_ref[0]) + sh_ref[0]
        h2_ref[0, rows, :] = h2
        hi = h2.astype(BF16)
        lo = (h2 - hi.astype(F32)).astype(BF16)
        pp = _dot(jnp.concatenate([hi, lo], axis=0), rw_ref[...])
        logits = (pp[:sub, :LANES] + pp[:sub, LANES:]) + (pp[sub:, :LANES] + pp[sub:, LANES:])
        lt = logits.T[0:N_EXPERTS, :]
        et = jnp.exp(lt - jnp.max(lt, axis=0, keepdims=True))
        afft_ref[0, :, rows] = et / jnp.sum(et, axis=0, keepdims=True)


def _outproj(x, yf, yb, ret, mla, win, mod, mrow, layer, n2, w_out, rw2, tm):
    b, l, d = x.shape

    def mspec(k):
        return pl.BlockSpec((1, 1, d), lambda bi, i: ((layer * MOD_ROWS + mrow(bi)) * 6 + k, 0, 0))

    tok = lambda w: pl.BlockSpec((1, tm, w), lambda bi, i: (bi, i, 0))
    const = lambda a: pl.BlockSpec(a.shape, lambda bi, i: (0,) * a.ndim)
    return pl.pallas_call(
        _outproj_kernel,
        out_shape=[jax.ShapeDtypeStruct((b, l, d), F32), jax.ShapeDtypeStruct((b, l, d), F32),
                   jax.ShapeDtypeStruct((b, N_EXPERTS, l), F32)],
        grid=(b, l // tm),
        in_specs=[tok(d), tok(256), tok(256), pl.BlockSpec((1, tm, 256), lambda bi, i: (bi, i, 3)),
                  tok(512), tok(256), mspec(2), mspec(3), mspec(4), const(n2), const(w_out),
                  const(rw2)],
        out_specs=[tok(d), tok(d), pl.BlockSpec((1, N_EXPERTS, tm), lambda bi, i: (bi, 0, i))],
        compiler_params=_cparams(("arbitrary", "arbitrary")),
        name="outproj",
    )(x, yf, yb, ret, mla, win, mod, mod, mod, n2, w_out, rw2)


def _route_kernel(afft_ref, tri_ref, tt_ref, idx_ref, gate_ref, cum_ref, *, cap):
    a = afft_ref[0]
    ne, l = a.shape
    as_f32 = lambda bits: lax.bitcast_convert_type(bits, F32)

    def search(_, c):
        lo, hi = c
        mid = lo + ((hi - lo) >> 1)
        cnt = jnp.sum(jnp.where(a >= as_f32(mid), 1.0, 0.0), axis=1, keepdims=True)
        ok = cnt >= cap
        return jnp.where(ok, mid, lo), jnp.where(ok, hi, mid)

    lo, hi = lax.fori_loop(0, 31, search, (jnp.zeros((ne, 1), jnp.int32),
                                           jnp.full((ne, 1), 0x7F800000, jnp.int32)))
    ge = jnp.where(a >= as_f32(lo), 1.0, 0.0)
    gt = jnp.where(a >= as_f32(hi), 1.0, 0.0)
    eq = ge - gt
    need = cap - jnp.sum(gt, axis=1, keepdims=True)

    def cumsum_to_scratch(mask):
        carry = jnp.zeros((ne, 1), F32)
        for j in range(l // LANES):
            cj = _dot(mask[:, j * LANES:(j + 1) * LANES].astype(BF16), tri_ref[...]) + carry
            cum_ref[:, j * LANES:(j + 1) * LANES] = cj
            carry = cj[:, LANES - 1:LANES]

    cumsum_to_scratch(eq)
    sel = gt + eq * jnp.where(cum_ref[...] - eq < need, 1.0, 0.0)
    cumsum_to_scratch(sel)
    cum_ref[...] = sel * cum_ref[...]
    slot = (lax.broadcasted_iota(jnp.int32, (cap, l), 0) + 1).astype(F32)

    def per_expert(e, _):
        onehot = jnp.where(cum_ref[pl.ds(e, 1), :] == slot, 1.0, 0.0).astype(BF16)
        a_e = afft_ref[0, pl.ds(e, 1), :]
        a1 = a_e.astype(BF16).astype(F32)
        a2 = (a_e - a1).astype(BF16).astype(F32)
        a3 = a_e - a1 - a2
        lhs = jnp.concatenate([tt_ref[0:2, :], a1, a2, a3, jnp.zeros((3, l), F32)], axis=0).astype(BF16)
        r = _dot_nt(lhs, onehot)
        idx_ref[0, pl.ds(e, 1), :, :] = (r[0:1, :] * LANES + r[1:2, :]).astype(jnp.int32).reshape(1, 1, cap)
        gate_ref[0, pl.ds(e, 1), :, :] = (r[2:3, :] + r[3:4, :] + r[4:5, :]).reshape(1, 1, cap)
        return 0

    lax.fori_loop(0, ne, per_expert, 0)


def _route(afft, cap):
    b, ne, l = afft.shape
    u = np.arange(LANES)
    tri = jnp.asarray(u[:, None] <= u[None, :], BF16)
    t = np.arange(l)
    tt = np.zeros((8, l), np.float32)
    tt[0], tt[1] = t // LANES, t % LANES
    tt = jnp.asarray(tt, F32)
    out_spec = pl.BlockSpec((1, ne, 1, cap), lambda bi: (bi, 0, 0, 0))
    return pl.pallas_call(
        functools.partial(_route_kernel, cap=cap),
        out_shape=[jax.ShapeDtypeStruct((b, ne, 1, cap), jnp.int32), jax.ShapeDtypeStruct((b, ne, 1, cap), F32)],
        grid=(b,),
        in_specs=[pl.BlockSpec((1, ne, l), lambda bi: (bi, 0, 0)),
                  pl.BlockSpec(tri.shape, lambda bi: (0, 0)), pl.BlockSpec(tt.shape, lambda bi: (0, 0))],
        out_specs=[out_spec, out_spec],
        scratch_shapes=[pltpu.VMEM((ne, l), F32)],
        compiler_params=_cparams(("arbitrary",)),
        name="route",
    )(afft, tri, tt)


def _ffn_kernel(idx_ref, h_ref, wg_ref, wu_ref, wd_ref, ys_ref, xa_ref, xb_ref, *, nb, cap, ne):
    e = pl.program_id(1)

    def gather_rolled(ex, dst):
        for s in range(nb):
            def body(j, _):
                dst[pl.ds(s * cap + j, 1), :] = h_ref[s, pl.ds(idx_ref[s, ex, 0, j], 1), :]
                return 0
            lax.fori_loop(0, cap, body, 0, unroll=8)

    def gather_unrolled(ex, dst):
        for s in range(nb):
            for j in range(cap):
                dst[s * cap + j:s * cap + j + 1, :] = h_ref[s, pl.ds(idx_ref[s, ex, 0, j], 1), :]

    def ffn(src):
        xb = src[...].astype(BF16)
        hid = _silu(_dot(xb, wg_ref[0])) * _dot(xb, wu_ref[0])
        y = _dot(hid.astype(BF16), wd_ref[0])
        for s in range(nb):
            ys_ref[s, 0] = y[s * cap:(s + 1) * cap, :]

    @pl.when(e == 0)
    def _():
        gather_rolled(0, xa_ref)

    e_next = jnp.minimum(e + 1, ne - 1)

    @pl.when(e % 2 == 0)
    def _():
        gather_unrolled(e_next, xb_ref)
        ffn(xa_ref)

    @pl.when(e % 2 == 1)
    def _():
        gather_unrolled(e_next, xa_ref)
        ffn(xb_ref)


def _expert_ffn(idx, h2, wg, wu, wd, nb):
    b, l, d = h2.shape
    ne, cap = idx.shape[1], idx.shape[3]
    ff = wg.shape[2]
    assert ne % 2 == 0
    return pl.pallas_call(
        functools.partial(_ffn_kernel, nb=nb, cap=cap, ne=ne),
        out_shape=jax.ShapeDtypeStruct((b, ne, cap, d), F32),
        grid=(b // nb, ne),
        in_specs=[pl.BlockSpec((nb, ne, 1, cap), lambda bi, e: (bi, 0, 0, 0), memory_space=pltpu.SMEM),
                  pl.BlockSpec((nb, l, d), lambda bi, e: (bi, 0, 0), pipeline_mode=pl.Buffered(1)),
                  pl.BlockSpec((1, d, ff), lambda bi, e: (e, 0, 0)),
                  pl.BlockSpec((1, d, ff), lambda bi, e: (e, 0, 0)),
                  pl.BlockSpec((1, ff, d), lambda bi, e: (e, 0, 0))],
        out_specs=pl.BlockSpec((nb, 1, cap, d), lambda bi, e: (bi, e, 0, 0)),
        scratch_shapes=[pltpu.VMEM((nb * cap, d), F32), pltpu.VMEM((nb * cap, d), F32)],
        compiler_params=_cparams(("arbitrary", "arbitrary")),
        name="expert_ffn",
    )(idx, h2, wg, wu, wd)


def _scatter_kernel(idx_ref, gate_ref, ys_ref, acc_ref, *, nb, cap):
    @pl.when(pl.program_id(1) == 0)
    def _():
        acc_ref[...] = jnp.zeros_like(acc_ref)

    grp = 8
    for s in range(nb):
        def add(gi, _):
            j0 = gi * grp
            ts = [idx_ref[s, 0, 0, j0 + k] for k in range(grp)]
            new = [acc_ref[s, pl.ds(ts[k], 1), :]
                   + ys_ref[s, 0, pl.ds(j0 + k, 1), :] * gate_ref[s, 0, 0, j0 + k] for k in range(grp)]
            for k in range(grp):
                acc_ref[s, pl.ds(ts[k], 1), :] = new[k]
            return 0
        lax.fori_loop(0, cap // grp, add, 0)


def _scatter(idx, gate, ys, l, nb):
    b, ne, cap, d = ys.shape
    smem = lambda: pl.BlockSpec((nb, 1, 1, cap), lambda bi, e: (bi, e, 0, 0), memory_space=pltpu.SMEM)
    return pl.pallas_call(
        functools.partial(_scatter_kernel, nb=nb, cap=cap),
        out_shape=jax.ShapeDtypeStruct((b, l, d), F32),
        grid=(b // nb, ne),
        in_specs=[smem(), smem(), pl.BlockSpec((nb, 1, cap, d), lambda bi, e: (bi, e, 0, 0))],
        out_specs=pl.BlockSpec((nb, l, d), lambda bi, e: (bi, 0, 0)),
        compiler_params=_cparams(("arbitrary", "arbitrary")),
        name="moe_scatter",
    )(idx, gate, ys)


def _final_kernel(x_ref, acc_ref, g5_ref, fg_ref, o_ref):
    o_ref[0] = _rms(x_ref[0] + g5_ref[0] * acc_ref[0], fg_ref[...])


def _final(x, acc, mod, layer, fg, tm):
    b, l, d = x.shape
    tok = pl.BlockSpec((1, tm, d), lambda bi, i: (bi, i, 0))
    return pl.pallas_call(
        _final_kernel,
        out_shape=jax.ShapeDtypeStruct((b, l, d), F32),
        grid=(b, l // tm),
        in_specs=[tok, tok, pl.BlockSpec((1, 1, d), lambda bi, i: ((layer * MOD_ROWS + bi) * 6 + 5, 0, 0)),
                  pl.BlockSpec(fg.shape, lambda bi, i: (0, 0))],
        out_specs=tok,
        compiler_params=_cparams(("arbitrary", "arbitrary")),
        name="final_norm",
    )(x, acc, mod, fg)


def _partner(dh):
    q = dh // 4
    return np.concatenate([np.arange(q, 2 * q), np.arange(0, q), np.arange(3 * q, 4 * q), np.arange(2 * q, 3 * q)])


def _rope_tables(n_tok, dh, reps):
    rows = n_tok // GRID_W
    row = jnp.repeat(jnp.arange(rows, dtype=jnp.int32), GRID_W)
    col = jnp.tile(jnp.arange(GRID_W, dtype=jnp.int32), rows)
    half = dh // 2
    inv = ROPE_BASE ** (-jnp.arange(0, half, 2, dtype=jnp.float32) / half)
    ar = row.astype(jnp.float32)[:, None] * inv[None, :]
    ac = col.astype(jnp.float32)[:, None] * inv[None, :]
    cos = jnp.concatenate([jnp.cos(ar), jnp.cos(ar), jnp.cos(ac), jnp.cos(ac)], axis=1)
    sin = jnp.concatenate([-jnp.sin(ar), jnp.sin(ar), -jnp.sin(ac), jnp.sin(ac)], axis=1)
    return jnp.tile(cos, (1, reps)), jnp.tile(sin, (1, reps))


def _extend_w_in(w):
    d = w.shape[0]
    wq, wk, wv = w[:, 1440:1696], w[:, 1696:1824], w[:, 1824:1952]
    kr = w[:, 1408:1440]
    pq = np.concatenate([h * WIN_DH + _partner(WIN_DH) for h in range(WIN_Q_HEADS)])
    pk = np.concatenate([h * WIN_DH + _partner(WIN_DH) for h in range(WIN_KV_HEADS)])
    ext = jnp.concatenate([w[:, 0:1408], wq, wq[:, pq], wk, wk[:, pk], wv, kr, kr[:, _partner(MLA_ROPE)],
                           jnp.zeros((d, N_EXT - C_KR - 2 * MLA_ROPE), w.dtype)], axis=1)
    return ext.astype(BF16)


def kernel(x, c, ctx, c_ctx, norm1_g, norm2_g, ada_w, ada_b, w_in, ret_decay_f, ret_decay_b,
           mla_qnorm_g, mla_kvnorm_g, mla_w_uq, mla_w_uk, mla_w_uv, win_sink, w_out,
           router_w, exp_w_gate, exp_w_up, exp_w_down, final_g):
    b, l, d = x.shape
    lc = ctx.shape[1]
    depth = w_in.shape[0]
    assert b + 1 <= MOD_ROWS and l % 512 == 0 and lc % RET_CHUNK == 0 and l >= 3 * WINDOW

    c_all = jnp.concatenate([c, c_ctx[None, :], jnp.zeros((MOD_ROWS - b - 1, d), F32)], axis=0)
    mod = _modulation(c_all, ada_w, ada_b).reshape(depth * MOD_ROWS * 6, 1, d)
    row_x = lambda bi: bi
    row_c = lambda bi: b

    wabs = _absorbed_q_weights(mla_w_uq, mla_w_uk)
    cw, sw = _rope_tables(l, WIN_DH, LANES // WIN_DH)
    cm, sm = _rope_tables(l, MLA_ROPE, 1)
    cq, sq = _rope_tables(l, MLA_ROPE, LANES // MLA_ROPE)
    pr = np.concatenate([h * MLA_ROPE + _partner(MLA_ROPE) for h in range(MLA_HEADS)])

    tm = 512
    acc_x = acc_c = None
    for layer in range(depth):
        need_ctx = layer < depth - 1
        w_ext = _extend_w_in(w_in[layer])
        g1 = norm1_g[layer].reshape(1, d)
        qng = mla_qnorm_g[layer].reshape(1, -1)
        kvg = mla_kvnorm_g[layer].reshape(1, -1)
        uq_rope = mla_w_uq[layer][:, :, MLA_NOPE:].reshape(MLA_Q_RANK, MLA_HEADS * MLA_ROPE)
        wq_all = jnp.concatenate([wabs[layer], uq_rope, uq_rope[:, pr]], axis=1).astype(BF16)
        wuv = jnp.transpose(mla_w_uv[layer], (1, 0, 2)).astype(BF16)
        dec = jnp.broadcast_to(jnp.concatenate([ret_decay_f[layer], ret_decay_b[layer]])[:, None],
                               (2 * RET_HEADS, LANES)).astype(F32)
        wo = w_out[layer].astype(BF16)
        n2 = norm2_g[layer].reshape(1, d)
        rw = jnp.pad(router_w[layer], ((0, 0), (0, LANES - N_EXPERTS)))
        rwh = rw.astype(BF16)
        rw2 = jnp.concatenate([rwh, (rw - rwh.astype(F32)).astype(BF16)], axis=1)
        wg, wu, wd = (exp_w_gate[layer].astype(BF16), exp_w_up[layer].astype(BF16),
                      exp_w_down[layer].astype(BF16))

        res_x = None if layer == 0 else (acc_x, layer - 1)
        res_c = None if layer == 0 else (acc_c, layer - 1)
        x, ret_x, cqn_x, kcat_x, win_x = _inproj(x, mod, row_x, layer, g1, w_ext, qng, kvg,
                                                 (cw, sw, cm, sm), res_x, tm)
        ctx, ret_c, cqn_c, kcat_c, win_c = _inproj(ctx, mod, row_c, layer, g1, w_ext, qng, kvg,
                                                   None, res_c, min(tm, lc))
        ycf, ycb, yxf, yxb = _retention(ret_c, ret_x, dec, 4 if l % (4 * RET_CHUNK) == 0 else 1)
        mla_x, wn_x = _attention(cqn_x, wq_all, (cq, sq), jnp.concatenate([kcat_c, kcat_x], axis=1), wuv,
                                 win_x, win_x, win_c, win_sink[layer])
        x, h2x, afft_x = _outproj(x, yxf, yxb, ret_x, mla_x, wn_x, mod, row_x, layer, n2, wo, rw2, tm)
        idx_x, gate_x = _route(afft_x, CAPACITY_FACTOR * l // N_EXPERTS)
        acc_x = _scatter(idx_x, gate_x, _expert_ffn(idx_x, h2x, wg, wu, wd, 1), l, 1)
        if need_ctx:
            mla_c, wn_c = _attention(cqn_c, wq_all, None, kcat_c, wuv, win_c, None, win_c, win_sink[layer])
            ctx, h2c, afft_c = _outproj(ctx, ycf, ycb, ret_c, mla_c, wn_c, mod, row_c, layer, n2, wo,
                                        rw2, min(tm, lc))
            idx_c, gate_c = _route(afft_c, CAPACITY_FACTOR * lc // N_EXPERTS)
            acc_c = _scatter(idx_c, gate_c, _expert_ffn(idx_c, h2c, wg, wu, wd, b), lc, b)
    return _final(x, acc_x, mod, depth - 1, final_g.reshape(1, d), tm)
```

```python
import functools

import numpy as np
import jax
import jax.numpy as jnp
from jax import lax
from jax.experimental import pallas as pl
from jax.experimental.pallas import tpu as pltpu

F32 = jnp.float32
BF16 = jnp.bfloat16

D_MODEL = 1024
DEPTH = 2
GRID_W = 64
RET_HEADS = 4
RET_DK = 64
RET_CHUNK = 128
MLA_HEADS = 8
MLA_Q_RANK = 256
MLA_KV_RANK = 128
MLA_NOPE = 64
MLA_ROPE = 32
MLA_DV = 64
WIN_Q_HEADS = 4
WIN_KV_HEADS = 2
WIN_DH = 64
WINDOW = 128
N_EXPERTS = 16
EXPERT_FF = 768
CAPACITY_FACTOR = 2
ROPE_BASE = 10000.0
NORM_EPS = 1e-6
GN_EPS = 1e-5
NEG_INF = -1e30
LOG2E = 1.4426950408889634

LANES = 128
MOD_ROWS = 16
ROUTE_BLOCKS = 32
KCAT = MLA_KV_RANK + MLA_ROPE
VMEM_LIMIT = 56 * 1024 * 1024

C_RET = 0
C_CQ = 1024
C_CKV = 1280
C_WQ = 1408
C_WQP = 1664
C_WK = 1920
C_WKP = 2048
C_WV = 2176
C_KR = 2304
N_EXT = 2432


def _cparams(sem):
    return pltpu.CompilerParams(dimension_semantics=sem, vmem_limit_bytes=VMEM_LIMIT)


def _dot(a, b):
    return jnp.dot(a, b, preferred_element_type=F32)


def _dot_nt(a, b):
    return lax.dot_general(a, b, (((1,), (1,)), ((), ())), preferred_element_type=F32)


def _dot_tn(a, b):
    return lax.dot_general(a, b, (((0,), (0,)), ((), ())), preferred_element_type=F32)


def _rms(x, g):
    return x * lax.rsqrt(jnp.mean(x * x, axis=-1, keepdims=True) + NORM_EPS) * g


def _silu(x):
    return x * jax.nn.sigmoid(x)


def _mod_kernel(c_ref, w_ref, b_ref, o_ref):
    o_ref[0] = jnp.dot(_silu(c_ref[...]), w_ref[0], preferred_element_type=F32,
                       precision=lax.Precision.HIGHEST) + b_ref[0]


def _modulation(c_all, ada_w, ada_b):
    depth, d, n = ada_w.shape
    tn = 1024
    return pl.pallas_call(
        _mod_kernel,
        out_shape=jax.ShapeDtypeStruct((depth, MOD_ROWS, n), F32),
        grid=(depth, n // tn),
        in_specs=[pl.BlockSpec((MOD_ROWS, d), lambda l, j: (0, 0)),
                  pl.BlockSpec((1, d, tn), lambda l, j: (l, 0, j)),
                  pl.BlockSpec((1, 1, tn), lambda l, j: (l, 0, j))],
        out_specs=pl.BlockSpec((1, MOD_ROWS, tn), lambda l, j: (l, 0, j)),
        compiler_params=_cparams(("arbitrary", "arbitrary")),
        name="adaln_mod",
    )(c_all, ada_w, ada_b.reshape(depth, 1, n))


def _wabs_kernel(uq_ref, uk_ref, o_ref):
    o_ref[0] = jnp.dot(uq_ref[0, 0], uk_ref[0, 0], preferred_element_type=F32,
                       precision=lax.Precision.HIGHEST)


def _absorbed_q_weights(w_uq, w_uk):
    depth = w_uq.shape[0]
    uq = jnp.transpose(w_uq[..., :MLA_NOPE], (0, 2, 1, 3))
    uk = jnp.transpose(w_uk, (0, 2, 3, 1))
    return pl.pallas_call(
        _wabs_kernel,
        out_shape=jax.ShapeDtypeStruct((depth, MLA_Q_RANK, MLA_HEADS * MLA_KV_RANK), F32),
        grid=(depth, MLA_HEADS),
        in_specs=[pl.BlockSpec((1, 1, MLA_Q_RANK, MLA_NOPE), lambda l, h: (l, h, 0, 0)),
                  pl.BlockSpec((1, 1, MLA_NOPE, MLA_KV_RANK), lambda l, h: (l, h, 0, 0))],
        out_specs=pl.BlockSpec((1, MLA_Q_RANK, MLA_KV_RANK), lambda l, h: (l, 0, h)),
        compiler_params=_cparams(("arbitrary", "arbitrary")),
        name="mla_absorb",
    )(uq, uk)


def _inproj_kernel(*refs, rope, resid):
    it = iter(refs)
    x_ref = next(it)
    if resid:
        acc_ref, g5_ref = next(it), next(it)
    g1_ref, sh_ref, sc_ref, w_ref, qng_ref, kvg_ref = (next(it) for _ in range(6))
    if rope:
        cw_ref, sw_ref, cm_ref, sm_ref = (next(it) for _ in range(4))
    if resid:
        xo_ref = next(it)
    ret_ref, cqn_ref, kcat_ref, win_ref = (next(it) for _ in range(4))

    x = x_ref[0]
    if resid:
        x = x + g5_ref[0] * acc_ref[0]
        xo_ref[0] = x
    h = _rms(x, g1_ref[...]) * (1.0 + sc_ref[0]) + sh_ref[0]
    p = _dot(h.astype(BF16), w_ref[...])

    ret_ref[0] = p[:, C_RET:C_CQ]
    cqn_ref[0] = _rms(p[:, C_CQ:C_CKV], qng_ref[...]).astype(BF16)
    kvn = _rms(p[:, C_CKV:C_WQ], kvg_ref[...])
    wq, wk, wv = p[:, C_WQ:C_WQP], p[:, C_WK:C_WKP], p[:, C_WV:C_KR]
    kr = p[:, C_KR:C_KR + MLA_ROPE]
    if rope:
        wqp, wkp = p[:, C_WQP:C_WK], p[:, C_WKP:C_WV]
        krp = p[:, C_KR + MLA_ROPE:C_KR + 2 * MLA_ROPE]
        cw, sw = cw_ref[...], sw_ref[...]
        wq = jnp.concatenate([wq[:, :LANES] * cw + wqp[:, :LANES] * sw,
                              wq[:, LANES:] * cw + wqp[:, LANES:] * sw], axis=1)
        wk = wk * cw + wkp * sw
        kr = kr * cm_ref[...] + krp * sm_ref[...]
    kcat_ref[0, :, 0:MLA_KV_RANK] = kvn.astype(BF16)
    kcat_ref[0, :, MLA_KV_RANK:KCAT] = kr.astype(BF16)
    win_ref[0, :, 0:256] = wq.astype(BF16)
    win_ref[0, :, 256:384] = wk.astype(BF16)
    win_ref[0, :, 384:512] = wv.astype(BF16)


def _inproj(x, mod, mrow, layer, g1, w_ext, qng, kvg, tables, resid, tm):
    b, l, d = x.shape
    rope = tables is not None

    def mspec(k, lay):
        return pl.BlockSpec((1, 1, d), lambda bi, i: ((lay * MOD_ROWS + mrow(bi)) * 6 + k, 0, 0))

    tok = lambda w: pl.BlockSpec((1, tm, w), lambda bi, i: (bi, i, 0))
    const = lambda a: pl.BlockSpec(a.shape, lambda bi, i: (0,) * a.ndim)
    args, specs = [x], [tok(d)]
    if resid is not None:
        args += [resid[0], mod]
        specs += [tok(d), mspec(5, resid[1])]
    args += [g1, mod, mod, w_ext, qng, kvg]
    specs += [const(g1), mspec(0, layer), mspec(1, layer), const(w_ext), const(qng), const(kvg)]
    if rope:
        for t in tables:
            args.append(t)
            specs.append(pl.BlockSpec((tm, t.shape[1]), lambda bi, i: (i, 0)))
    out_shape, out_specs = [], []
    if resid is not None:
        out_shape.append(jax.ShapeDtypeStruct((b, l, d), F32))
        out_specs.append(tok(d))
    out_shape += [jax.ShapeDtypeStruct((b, l, 1024), F32), jax.ShapeDtypeStruct((b, l, MLA_Q_RANK), BF16),
                  jax.ShapeDtypeStruct((b, l, KCAT), BF16), jax.ShapeDtypeStruct((b, l, 512), BF16)]
    out_specs += [tok(1024), tok(MLA_Q_RANK), tok(KCAT), tok(512)]
    outs = pl.pallas_call(
        functools.partial(_inproj_kernel, rope=rope, resid=resid is not None),
        out_shape=out_shape, grid=(b, l // tm), in_specs=specs, out_specs=out_specs,
        compiler_params=_cparams(("arbitrary", "arbitrary")),
        name="inproj",
    )(*args)
    if resid is None:
        return (x,) + tuple(outs)
    return tuple(outs)


def _ret_kernel(dec_ref, c_ref, xf_ref, xb_ref, ycf_ref, ycb_ref, yxf_ref, yxb_ref,
                s_ref, dm_ref, qw_ref, kw_ref, gc_ref):
    n = pl.program_id(1)
    c = RET_CHUNK
    dk = RET_DK

    @pl.when(n == 0)
    def _():
        s_ref[...] = jnp.zeros_like(s_ref)
        lg = jnp.log1p(-jnp.exp2(dec_ref[...]))
        ii = lax.broadcasted_iota(jnp.int32, (c, c), 0).astype(F32)
        jj = lax.broadcasted_iota(jnp.int32, (c, c), 1).astype(F32)
        ir = lax.broadcasted_iota(jnp.int32, (c, dk), 0).astype(F32)
        for h in range(RET_HEADS):
            lf = lg[h:h + 1, :]
            lb = lg[RET_HEADS + h:RET_HEADS + h + 1, :]
            keep_f = ii >= jj
            dm_ref[h] = jnp.where(keep_f, jnp.exp(lf * jnp.where(keep_f, ii - jj, 0.0)), 0.0)
            keep_b = jj > ii
            dm_ref[RET_HEADS + h] = jnp.where(keep_b, jnp.exp(lb * jnp.where(keep_b, jj - ii, 0.0)), 0.0)
            qw_ref[h] = jnp.exp(lf[:, :dk] * (ir + 1.0))
            kw_ref[h] = jnp.exp(lf[:, :dk] * (c - 1.0 - ir))
            qw_ref[RET_HEADS + h] = jnp.exp(lb[:, :dk] * (c - ir))
            kw_ref[RET_HEADS + h] = jnp.exp(lb[:, :dk] * ir)
            gc_ref[h] = jnp.exp(lf[:, :dk] * float(c))
            gc_ref[RET_HEADS + h] = jnp.exp(lb[:, :dk] * float(c))

    def run(src_ref, dst_ref, d):
        n_sub = src_ref.shape[1] // c
        order = range(n_sub) if d == 0 else range(n_sub - 1, -1, -1)
        for h in range(RET_HEADS):
            r = d * RET_HEADS + h
            intra, kv, qd = {}, {}, {}
            for g in order:
                rows = slice(g * c, (g + 1) * c)
                q = src_ref[0, rows, h * dk:(h + 1) * dk]
                k = src_ref[0, rows, 256 + h * dk:256 + (h + 1) * dk] * (RET_DK ** -0.5)
                v = src_ref[0, rows, 512 + h * dk:512 + (h + 1) * dk].astype(BF16)
                sc = _dot_nt(q.astype(BF16), k.astype(BF16)) * dm_ref[r]
                intra[g] = _dot(sc.astype(BF16), v)
                kv[g] = _dot_tn((k * kw_ref[r]).astype(BF16), v)
                qd[g] = (q * qw_ref[r]).astype(BF16)
            s = s_ref[r]
            for g in order:
                dst_ref[0, g * c:(g + 1) * c, h * dk:(h + 1) * dk] = intra[g] + _dot(qd[g], s.astype(BF16))
                s = gc_ref[r] * s + kv[g]
            s_ref[r] = s

    @pl.when(n == 0)
    def _():
        run(c_ref, ycf_ref, 0)
        run(c_ref, ycb_ref, 1)

    @pl.when(n > 0)
    def _():
        run(xf_ref, yxf_ref, 0)
        run(xb_ref, yxb_ref, 1)


def _retention(ret_c, ret_x, dec, g_sub):
    b, lc, _ = ret_c.shape
    lx = ret_x.shape[1]
    c = RET_CHUNK
    blk = g_sub * c
    assert lc % c == 0 and lx % blk == 0
    nx = lx // blk
    ci = lambda bi, n: (bi, 0, 0)
    xfi = lambda bi, n: (bi, jnp.maximum(n - 1, 0), 0)
    xbi = lambda bi, n: (bi, nx - 1 - jnp.maximum(n - 1, 0), 0)
    ydim = RET_HEADS * RET_DK
    return pl.pallas_call(
        _ret_kernel,
        out_shape=[jax.ShapeDtypeStruct((b, lc, ydim), F32), jax.ShapeDtypeStruct((b, lc, ydim), F32),
                   jax.ShapeDtypeStruct((b, lx, ydim), F32), jax.ShapeDtypeStruct((b, lx, ydim), F32)],
        grid=(b, 1 + nx),
        in_specs=[pl.BlockSpec(dec.shape, lambda bi, n: (0, 0)),
                  pl.BlockSpec((1, lc, 768), ci),
                  pl.BlockSpec((1, blk, 768), xfi), pl.BlockSpec((1, blk, 768), xbi)],
        out_specs=[pl.BlockSpec((1, lc, ydim), ci), pl.BlockSpec((1, lc, ydim), ci),
                   pl.BlockSpec((1, blk, ydim), xfi), pl.BlockSpec((1, blk, ydim), xbi)],
        scratch_shapes=[pltpu.VMEM((2 * RET_HEADS, RET_DK, RET_DK), F32),
                        pltpu.VMEM((2 * RET_HEADS, c, c), F32),
                        pltpu.VMEM((2 * RET_HEADS, c, RET_DK), F32),
                        pltpu.VMEM((2 * RET_HEADS, c, RET_DK), F32),
                        pltpu.VMEM((2 * RET_HEADS, 1, RET_DK), F32)],
        compiler_params=_cparams(("arbitrary", "arbitrary")),
        name="retention",
    )(dec, ret_c, ret_x, ret_x)


def _mla_kernel(*refs, rope, band, tq):
    it = iter(refs)
    cqn_ref, wq_ref = next(it), next(it)
    if rope:
        cq_ref, sq_ref = next(it), next(it)
    k_ref, wuv_ref, sink_ref, wnq_ref = (next(it) for _ in range(4))
    wkx_ref, wvx_ref = (next(it), next(it)) if band else (None, None)
    wkc_ref, wvc_ref, o_ref, wno_ref, qcat_ref, s_ref, m_ref = (next(it) for _ in range(7))

    nh = MLA_HEADS
    tk = k_ref.shape[1] // 2
    scale = (MLA_NOPE + MLA_ROPE) ** -0.5 * LOG2E
    qa = _dot(cqn_ref[0], wq_ref[...]) * scale
    qr = qa[:, nh * MLA_KV_RANK:nh * MLA_KV_RANK + nh * MLA_ROPE]
    if rope:
        qp = qa[:, nh * MLA_KV_RANK + nh * MLA_ROPE:]
        cq, sq = cq_ref[...], sq_ref[...]
        qr = jnp.concatenate([qr[:, :LANES] * cq + qp[:, :LANES] * sq,
                              qr[:, LANES:] * cq + qp[:, LANES:] * sq], axis=1)
    for h in range(nh):
        qcat_ref[h * tq:(h + 1) * tq, 0:MLA_KV_RANK] = qa[:, h * MLA_KV_RANK:(h + 1) * MLA_KV_RANK].astype(BF16)
        qcat_ref[h * tq:(h + 1) * tq, MLA_KV_RANK:KCAT] = qr[:, h * MLA_ROPE:(h + 1) * MLA_ROPE].astype(BF16)
    once = jnp.minimum(pl.program_id(1) + 1, 1)

    def run_once(fn):
        lax.fori_loop(0, once, lambda i, c: (fn(), c)[1], 0)

    def pass1():
        q = qcat_ref[...]
        mx = None
        for c in range(2):
            s = _dot_nt(q, k_ref[0, c * tk:(c + 1) * tk, :])
            s_ref[c] = s
            for j in range(0, tk, LANES):
                mx = s[:, j:j + LANES] if mx is None else jnp.maximum(mx, s[:, j:j + LANES])
        m_ref[...] = jnp.broadcast_to(jnp.max(mx, axis=-1, keepdims=True), m_ref.shape)
        _window_block(pl.program_id(1), sink_ref, wnq_ref, wkx_ref, wvx_ref, wkc_ref, wvc_ref, wno_ref)

    def pass2():
        m = m_ref[:, 0:1]
        ones_col = jnp.where(lax.broadcasted_iota(jnp.int32, (tk, LANES), 1) == 0, 1.0, 0.0).astype(BF16)

        def part(c):
            v_aug = jnp.concatenate([k_ref[0, c * tk:(c + 1) * tk, 0:MLA_KV_RANK], ones_col], axis=1)
            return _dot(jnp.exp2(s_ref[c] - m).astype(BF16), v_aug)

        acc = part(0) + part(1)
        o = (acc[:, 0:MLA_KV_RANK] * (1.0 / acc[:, MLA_KV_RANK:MLA_KV_RANK + 1])).astype(BF16)
        for h in range(nh):
            o_ref[0, :, h * MLA_DV:(h + 1) * MLA_DV] = _dot(o[h * tq:(h + 1) * tq, :], wuv_ref[h]).astype(BF16)

    run_once(pass1)
    run_once(pass2)


def _attention(cqn, wq_all, tables, keys, wuv, win_q, win_x, win_c, sink):
    b, l, _ = cqn.shape
    lk = keys.shape[1]
    lc = win_c.shape[1]
    tq = WINDOW
    rope = tables is not None
    band = win_x is not None
    kvw = WIN_KV_HEADS * WIN_DH
    assert lk % (2 * LANES) == 0 and l % tq == 0
    args = [cqn, wq_all]
    specs = [pl.BlockSpec((1, tq, MLA_Q_RANK), lambda bi, i: (bi, i, 0)),
             pl.BlockSpec(wq_all.shape, lambda bi, i: (0, 0))]
    if rope:
        for t in tables:
            args.append(t)
            specs.append(pl.BlockSpec((tq, LANES), lambda bi, i: (i, 0)))
    args.append(keys)
    specs.append(pl.BlockSpec((1, lk, KCAT), lambda bi, i: (bi, 0, 0)))
    args += [wuv, sink, win_q]
    specs += [pl.BlockSpec(wuv.shape, lambda bi, i: (0, 0, 0)), pl.BlockSpec(memory_space=pltpu.SMEM),
              pl.BlockSpec((1, tq, 256), lambda bi, i: (bi, i, 0))]
    if band:
        args += [win_x, win_x]
        specs += [pl.BlockSpec((1, l, kvw), lambda bi, i: (bi, 0, 2)),
                  pl.BlockSpec((1, l, kvw), lambda bi, i: (bi, 0, 3))]
    args += [win_c, win_c]
    specs += [pl.BlockSpec((1, lc, kvw), lambda bi, i: (bi, 0, 2)),
              pl.BlockSpec((1, lc, kvw), lambda bi, i: (bi, 0, 3))]
    rows = MLA_HEADS * tq
    scratch = [pltpu.VMEM((rows, KCAT), BF16), pltpu.VMEM((2, rows, lk // 2), F32),
               pltpu.VMEM((rows, LANES), F32)]
    return pl.pallas_call(
        functools.partial(_mla_kernel, rope=rope, band=band, tq=tq),
        out_shape=[jax.ShapeDtypeStruct((b, l, MLA_HEADS * MLA_DV), BF16),
                   jax.ShapeDtypeStruct((b, l, WIN_Q_HEADS * WIN_DH), BF16)],
        grid=(b, l // tq), in_specs=specs,
        out_specs=[pl.BlockSpec((1, tq, MLA_HEADS * MLA_DV), lambda bi, i: (bi, i, 0)),
                   pl.BlockSpec((1, tq, WIN_Q_HEADS * WIN_DH), lambda bi, i: (bi, i, 0))],
        scratch_shapes=scratch,
        compiler_params=_cparams(("arbitrary", "arbitrary")),
        name="attention",
    )(*args)


def _window_block(n, sink_ref, q_ref, kx_ref, vx_ref, kc_ref, vc_ref, o_ref):
    band = kx_ref is not None
    w = WINDOW
    d = WIN_DH
    g = WIN_Q_HEADS // WIN_KV_HEADS
    scale = d ** -0.5
    q = q_ref[0]
    rows = g * w
    row_id = lax.broadcasted_iota(jnp.int32, (rows, 1), 0)
    if band:
        l_x = kx_ref.shape[1]
        start = pl.multiple_of(jnp.clip((n - 1) * w, 0, l_x - 3 * w), w)
        qi = n * w + lax.broadcasted_iota(jnp.int32, (rows, 3 * w), 0) % w
        kj = start + lax.broadcasted_iota(jnp.int32, (rows, 3 * w), 1)
        keep = jnp.abs(kj - qi) <= WINDOW
        kb_all = kx_ref[0, pl.ds(start, 3 * w), :]
        vb_all = vx_ref[0, pl.ds(start, 3 * w), :]
    kc_all, vc_all = kc_ref[0], vc_ref[0]
    for j in range(WIN_KV_HEADS):
        q2 = jnp.concatenate([q[:, (g * j + t) * d:(g * j + t + 1) * d] for t in range(g)], axis=0)
        sk = jnp.zeros((rows, 1), F32)
        for t in range(g):
            sk = jnp.where(row_id // w == t, sink_ref[g * j + t], sk)
        s_ctx = _dot_nt(q2, kc_all[:, j * d:(j + 1) * d]) * scale
        m = jnp.maximum(jnp.max(s_ctx, axis=-1, keepdims=True), sk)
        if band:
            s_loc = _dot_nt(q2, kb_all[:, j * d:(j + 1) * d]) * scale
            s_loc = jnp.where(keep, s_loc, NEG_INF)
            m = jnp.maximum(m, jnp.max(s_loc, axis=-1, keepdims=True))
        p_ctx = jnp.exp(s_ctx - m)
        l = jnp.sum(p_ctx, axis=-1, keepdims=True) + jnp.exp(sk - m)
        o = _dot(p_ctx.astype(BF16), vc_all[:, j * d:(j + 1) * d])
        if band:
            p_loc = jnp.exp(s_loc - m)
            l = l + jnp.sum(p_loc, axis=-1, keepdims=True)
            o = o + _dot(p_loc.astype(BF16), vb_all[:, j * d:(j + 1) * d])
        o = (o * (1.0 / l)).astype(BF16)
        for t in range(g):
            o_ref[0, :, (g * j + t) * d:(g * j + t + 1) * d] = o[t * w:(t + 1) * w, :]


def _outproj_kernel(x_ref, yf_ref, yb_ref, rg_ref, mla_ref, win_ref, g2_ref, sh_ref, sc_ref,
                    n2_ref, wo_ref, rw_ref, xo_ref, h2_ref, afft_ref):
    tm = x_ref.shape[1]
    sub = 128 if tm % 128 == 0 else tm
    for r0 in range(0, tm, sub):
        rows = slice(r0, r0 + sub)
        y = yf_ref[0, rows, :] + yb_ref[0, rows, :]
        parts = []
        for h in range(RET_HEADS):
            yh = y[:, h * RET_DK:(h + 1) * RET_DK]
            mu = jnp.mean(yh, axis=-1, keepdims=True)
            var = jnp.mean(jnp.square(yh - mu), axis=-1, keepdims=True)
            parts.append((yh - mu) * lax.rsqrt(var + GN_EPS))
        ret = jnp.concatenate(parts, axis=1) * _silu(rg_ref[0, rows, :])
        proj = (_dot(ret.astype(BF16), wo_ref[0:256, :]) + _dot(mla_ref[0, rows, :], wo_ref[256:768, :])
                + _dot(win_ref[0, rows, :], wo_ref[768:1024, :]))
        x = x_ref[0, rows, :] + g2_ref[0] * proj
        xo_ref[0, rows, :] = x
        h2 = _rms(x, n2_ref[...]) * (1.0 + sc_ref[0]) + sh_ref[0]
        h2_ref[0, rows, :] = h2
        hi = h2.astype(BF16)
        lo = (h2 - hi.astype(F32)).astype(BF16)
        pp = _dot(jnp.concatenate([hi, lo], axis=0), rw_ref[...])
        logits = (pp[:sub, :LANES] + pp[:sub, LANES:]) + (pp[sub:, :LANES] + pp[sub:, LANES:])
        lt = logits.T[0:N_EXPERTS, :]
        et = jnp.exp(lt - jnp.max(lt, axis=0, keepdims=True))
        afft_ref[0, :, rows] = et / jnp.sum(et, axis=0, keepdims=True)


def _outproj(x, yf, yb, ret, mla, win, mod, mrow, layer, n2, w_out, rw2, tm):
    b, l, d = x.shape

    def mspec(k):
        return pl.BlockSpec((1, 1, d), lambda bi, i: ((layer * MOD_ROWS + mrow(bi)) * 6 + k, 0, 0))

    tok = lambda w: pl.BlockSpec((1, tm, w), lambda bi, i: (bi, i, 0))
    const = lambda a: pl.BlockSpec(a.shape, lambda bi, i: (0,) * a.ndim)
    return pl.pallas_call(
        _outproj_kernel,
        out_shape=[jax.ShapeDtypeStruct((b, l, d), F32), jax.ShapeDtypeStruct((b, l, d), F32),
                   jax.ShapeDtypeStruct((b, N_EXPERTS, l), F32)],
        grid=(b, l // tm),
        in_specs=[tok(d), tok(256), tok(256), pl.BlockSpec((1, tm, 256), lambda bi, i: (bi, i, 3)),
                  tok(512), tok(256), mspec(2), mspec(3), mspec(4), const(n2), const(w_out),
                  const(rw2)],
        out_specs=[tok(d), tok(d), pl.BlockSpec((1, N_EXPERTS, tm), lambda bi, i: (bi, 0, i))],
        compiler_params=_cparams(("arbitrary", "arbitrary")),
        name="outproj",
    )(x, yf, yb, ret, mla, win, mod, mod, mod, n2, w_out, rw2)


def _route_kernel(a_ref, tri_ref, bd_ref, idx_ref, place_ref, exc_ref, inc_ref, *, cap):
    a = a_ref[...]
    ne, nb, _ = a.shape
    grp = bd_ref.shape[0]
    as_f32 = lambda bits: lax.bitcast_convert_type(bits, F32)
    count = lambda m: jnp.sum(jnp.sum(m, axis=1, keepdims=True), axis=2, keepdims=True)

    def earlier_blocks(t):
        tb = t.astype(BF16)
        return jnp.concatenate([_dot(bd_ref[...], tb[r:r + grp, :]) for r in range(0, ne * nb, grp)], axis=0)

    def search(_, c):
        lo, hi = c
        mid = lo + ((hi - lo) >> 1)
        ok = count(jnp.where(a >= as_f32(mid), 1.0, 0.0)) >= cap
        return jnp.where(ok, mid, lo), jnp.where(ok, hi, mid)

    lo, hi = lax.fori_loop(0, 31, search, (jnp.zeros((ne, 1, 1), jnp.int32),
                                           jnp.full((ne, 1, 1), 0x7F800000, jnp.int32)))
    gt3 = jnp.where(a >= as_f32(hi), 1.0, 0.0)
    eq = (jnp.where(a >= as_f32(lo), 1.0, 0.0) - gt3).reshape(ne * nb, LANES)
    gt = gt3.reshape(ne * nb, LANES)
    need = jnp.broadcast_to(cap - count(gt3), a.shape).reshape(ne * nb, LANES)

    cum_eq = _dot(eq.astype(BF16), tri_ref[...])
    tot_eq = jnp.broadcast_to(cum_eq[:, LANES - 1:LANES], cum_eq.shape)
    before = earlier_blocks(tot_eq) + cum_eq - eq
    sel = gt + eq * jnp.where(before < need, 1.0, 0.0)
    selb = sel.astype(BF16)
    cum = _dot(selb, tri_ref[...])
    tot = jnp.broadcast_to(cum[:, LANES - 1:LANES], cum.shape)
    place = sel * (earlier_blocks(tot) + cum)
    high = jnp.floor(place * (1.0 / LANES))
    place_ref[:, 0:LANES] = high
    place_ref[:, LANES:2 * LANES] = place - high * LANES
    earlier = earlier_blocks(sel)
    exc_ref[...] = earlier
    inc_ref[...] = earlier + sel

    slot = lax.broadcasted_iota(jnp.int32, (cap, nb), 0).astype(F32)
    slot1 = (lax.broadcasted_iota(jnp.int32, (cap, LANES), 0) + 1).astype(F32)
    ones8 = jnp.ones((8, LANES), BF16)
    ids = jnp.concatenate([lax.broadcasted_iota(jnp.int32, (8, LANES), 1),
                           lax.broadcasted_iota(jnp.int32, (8, nb), 1) * LANES], axis=1).astype(BF16)

    def per_expert(e, _):
        rows = pl.ds(pl.multiple_of(e * nb, nb), nb)
        inc = _dot_nt(ones8, inc_ref[rows, :].astype(BF16))[0:1, :]
        exc = _dot_nt(ones8, exc_ref[rows, :].astype(BF16))[0:1, :]
        blk = jnp.where(slot >= exc, 1.0, 0.0) * jnp.where(slot < inc, 1.0, 0.0)
        got = _dot(blk.astype(BF16), place_ref[rows, :].astype(BF16))
        match = jnp.where(got[:, 0:LANES] * LANES + got[:, LANES:] == slot1, 1.0, 0.0)
        tok = _dot_nt(ids, jnp.concatenate([match, blk], axis=1).astype(BF16))
        idx_ref[pl.ds(e, 1), :, :] = tok[0:1, :].astype(jnp.int32).reshape(1, 1, cap)
        return 0

    lax.fori_loop(0, ne, per_expert, 0, unroll=4)


def _route(afft, cap):
    b, ne, l = afft.shape
    nb = max(l // LANES, ROUTE_BLOCKS)
    g = b * ne
    a3 = jnp.pad(afft, ((0, 0), (0, 0), (0, nb * LANES - l))).reshape(g, nb, LANES)
    u = np.arange(LANES)
    tri = jnp.asarray(u[:, None] <= u[None, :], BF16)
    r = np.arange(ne * nb)
    bd = jnp.asarray((r[:, None] // nb == r[None, :] // nb) & (r[None, :] % nb < r[:, None] % nb), BF16)
    const = lambda x: pl.BlockSpec(x.shape, lambda i: (0,) * x.ndim)
    idx = pl.pallas_call(
        functools.partial(_route_kernel, cap=cap),
        out_shape=jax.ShapeDtypeStruct((g, 1, cap), jnp.int32),
        grid=(1,),
        in_specs=[const(a3), const(tri), const(bd)],
        out_specs=pl.BlockSpec((g, 1, cap), lambda i: (0, 0, 0)),
        scratch_shapes=[pltpu.VMEM((g * nb, 2 * LANES), F32), pltpu.VMEM((g * nb, LANES), F32),
                        pltpu.VMEM((g * nb, LANES), F32)],
        compiler_params=_cparams(("arbitrary",)),
        name="route",
    )(a3, tri, bd)
    return idx.reshape(b, ne, 1, cap)


def _ffn_kernel(idx_ref, h_ref, wg_ref, wu_ref, wd_ref, ys_ref, xa_ref, xb_ref, *, nb, cap, ne):
    e = pl.program_id(1)

    def gather_rolled(ex, dst):
        for s in range(nb):
            def body(j, _):
                dst[pl.ds(s * cap + j, 1), :] = h_ref[s, pl.ds(idx_ref[s, ex, 0, j], 1), :]
                return 0
            lax.fori_loop(0, cap, body, 0, unroll=8)

    def gather_unrolled(ex, dst):
        for s in range(nb):
            for j in range(cap):
                dst[s * cap + j:s * cap + j + 1, :] = h_ref[s, pl.ds(idx_ref[s, ex, 0, j], 1), :]

    def ffn(src):
        xb = src[...].astype(BF16)
        hid = _silu(_dot(xb, wg_ref[0])) * _dot(xb, wu_ref[0])
        y = _dot(hid.astype(BF16), wd_ref[0])
        for s in range(nb):
            ys_ref[s, 0] = y[s * cap:(s + 1) * cap, :]

    @pl.when(e == 0)
    def _():
        gather_rolled(0, xa_ref)

    e_next = jnp.minimum(e + 1, ne - 1)

    @pl.when(e % 2 == 0)
    def _():
        gather_unrolled(e_next, xb_ref)
        ffn(xa_ref)

    @pl.when(e % 2 == 1)
    def _():
        gather_unrolled(e_next, xa_ref)
        ffn(xb_ref)


def _expert_ffn(idx, h2, wg, wu, wd, nb):
    b, l, d = h2.shape
    ne, cap = idx.shape[1], idx.shape[3]
    ff = wg.shape[2]
    assert ne % 2 == 0
    return pl.pallas_call(
        functools.partial(_ffn_kernel, nb=nb, cap=cap, ne=ne),
        out_shape=jax.ShapeDtypeStruct((b, ne, cap, d), F32),
        grid=(b // nb, ne),
        in_specs=[pl.BlockSpec((nb, ne, 1, cap), lambda bi, e: (bi, 0, 0, 0), memory_space=pltpu.SMEM),
                  pl.BlockSpec((nb, l, d), lambda bi, e: (bi, 0, 0), pipeline_mode=pl.Buffered(1)),
                  pl.BlockSpec((1, d, ff), lambda bi, e: (e, 0, 0)),
                  pl.BlockSpec((1, d, ff), lambda bi, e: (e, 0, 0)),
                  pl.BlockSpec((1, ff, d), lambda bi, e: (e, 0, 0))],
        out_specs=pl.BlockSpec((nb, 1, cap, d), lambda bi, e: (bi, e, 0, 0)),
        scratch_shapes=[pltpu.VMEM((nb * cap, d), F32), pltpu.VMEM((nb * cap, d), F32)],
        compiler_params=_cparams(("arbitrary", "arbitrary")),
        name="expert_ffn",
    )(idx, h2, wg, wu, wd)


def _scatter_kernel(idx_ref, aff_ref, ys_ref, acc_ref, *, nb, cap):
    @pl.when(pl.program_id(1) == 0)
    def _():
        acc_ref[...] = jnp.zeros_like(acc_ref)

    grp = 8
    for s in range(nb):
        def add(gi, _):
            j0 = gi * grp
            ts = [idx_ref[s, 0, 0, j0 + k] for k in range(grp)]
            new = [acc_ref[s, pl.ds(ts[k], 1), :]
                   + ys_ref[s, 0, pl.ds(j0 + k, 1), :] * aff_ref[s, 0, 0, ts[k]] for k in range(grp)]
            for k in range(grp):
                acc_ref[s, pl.ds(ts[k], 1), :] = new[k]
            return 0
        lax.fori_loop(0, cap // grp, add, 0)


def _scatter(idx, afft, ys, nb):
    b, ne, cap, d = ys.shape
    l = afft.shape[2]
    smem = lambda w: pl.BlockSpec((nb, 1, 1, w), lambda bi, e: (bi, e, 0, 0), memory_space=pltpu.SMEM)
    return pl.pallas_call(
        functools.partial(_scatter_kernel, nb=nb, cap=cap),
        out_shape=jax.ShapeDtypeStruct((b, l, d), F32),
        grid=(b // nb, ne),
        in_specs=[smem(cap), smem(l), pl.BlockSpec((nb, 1, cap, d), lambda bi, e: (bi, e, 0, 0))],
        out_specs=pl.BlockSpec((nb, l, d), lambda bi, e: (bi, 0, 0)),
        compiler_params=_cparams(("arbitrary", "arbitrary")),
        name="moe_scatter",
    )(idx, afft.reshape(b, ne, 1, l), ys)


def _final_kernel(x_ref, acc_ref, g5_ref, fg_ref, o_ref):
    o_ref[0] = _rms(x_ref[0] + g5_ref[0] * acc_ref[0], fg_ref[...])


def _final(x, acc, mod, layer, fg, tm):
    b, l, d = x.shape
    tok = pl.BlockSpec((1, tm, d), lambda bi, i: (bi, i, 0))
    return pl.pallas_call(
        _final_kernel,
        out_shape=jax.ShapeDtypeStruct((b, l, d), F32),
        grid=(b, l // tm),
        in_specs=[tok, tok, pl.BlockSpec((1, 1, d), lambda bi, i: ((layer * MOD_ROWS + bi) * 6 + 5, 0, 0)),
                  pl.BlockSpec(fg.shape, lambda bi, i: (0, 0))],
        out_specs=tok,
        compiler_params=_cparams(("arbitrary", "arbitrary")),
        name="final_norm",
    )(x, acc, mod, fg)


def _partner(dh):
    q = dh // 4
    return np.concatenate([np.arange(q, 2 * q), np.arange(0, q), np.arange(3 * q, 4 * q), np.arange(2 * q, 3 * q)])


def _rope_tables(n_tok, dh, reps):
    rows = n_tok // GRID_W
    row = jnp.repeat(jnp.arange(rows, dtype=jnp.int32), GRID_W)
    col = jnp.tile(jnp.arange(GRID_W, dtype=jnp.int32), rows)
    half = dh // 2
    inv = ROPE_BASE ** (-jnp.arange(0, half, 2, dtype=jnp.float32) / half)
    ar = row.astype(jnp.float32)[:, None] * inv[None, :]
    ac = col.astype(jnp.float32)[:, None] * inv[None, :]
    cos = jnp.concatenate([jnp.cos(ar), jnp.cos(ar), jnp.cos(ac), jnp.cos(ac)], axis=1)
    sin = jnp.concatenate([-jnp.sin(ar), jnp.sin(ar), -jnp.sin(ac), jnp.sin(ac)], axis=1)
    return jnp.tile(cos, (1, reps)), jnp.tile(sin, (1, reps))


def _extend_w_in(w):
    d = w.shape[0]
    wq, wk, wv = w[:, 1440:1696], w[:, 1696:1824], w[:, 1824:1952]
    kr = w[:, 1408:1440]
    pq = np.concatenate([h * WIN_DH + _partner(WIN_DH) for h in range(WIN_Q_HEADS)])
    pk = np.concatenate([h * WIN_DH + _partner(WIN_DH) for h in range(WIN_KV_HEADS)])
    ext = jnp.concatenate([w[:, 0:1408], wq, wq[:, pq], wk, wk[:, pk], wv, kr, kr[:, _partner(MLA_ROPE)],
                           jnp.zeros((d, N_EXT - C_KR - 2 * MLA_ROPE), w.dtype)], axis=1)
    return ext.astype(BF16)


def kernel(x, c, ctx, c_ctx, norm1_g, norm2_g, ada_w, ada_b, w_in, ret_decay_f, ret_decay_b,
           mla_qnorm_g, mla_kvnorm_g, mla_w_uq, mla_w_uk, mla_w_uv, win_sink, w_out,
           router_w, exp_w_gate, exp_w_up, exp_w_down, final_g):
    b, l, d = x.shape
    lc = ctx.shape[1]
    depth = w_in.shape[0]
    assert b + 1 <= MOD_ROWS and l % 512 == 0 and lc % RET_CHUNK == 0 and l >= 3 * WINDOW

    c_all = jnp.concatenate([c, c_ctx[None, :], jnp.zeros((MOD_ROWS - b - 1, d), F32)], axis=0)
    mod = _modulation(c_all, ada_w, ada_b).reshape(depth * MOD_ROWS * 6, 1, d)
    row_x = lambda bi: bi
    row_c = lambda bi: b

    wabs = _absorbed_q_weights(mla_w_uq, mla_w_uk)
    cw, sw = _rope_tables(l, WIN_DH, LANES // WIN_DH)
    cm, sm = _rope_tables(l, MLA_ROPE, 1)
    cq, sq = _rope_tables(l, MLA_ROPE, LANES // MLA_ROPE)
    pr = np.concatenate([h * MLA_ROPE + _partner(MLA_ROPE) for h in range(MLA_HEADS)])

    tm = 512
    acc_x = acc_c = None
    for layer in range(depth):
        need_ctx = layer < depth - 1
        w_ext = _extend_w_in(w_in[layer])
        g1 = norm1_g[layer].reshape(1, d)
        qng = mla_qnorm_g[layer].reshape(1, -1)
        kvg = mla_kvnorm_g[layer].reshape(1, -1)
        uq_rope = mla_w_uq[layer][:, :, MLA_NOPE:].reshape(MLA_Q_RANK, MLA_HEADS * MLA_ROPE)
        wq_all = jnp.concatenate([wabs[layer], uq_rope, uq_rope[:, pr]], axis=1).astype(BF16)
        wuv = jnp.transpose(mla_w_uv[layer], (1, 0, 2)).astype(BF16)
        dec = jnp.broadcast_to(jnp.concatenate([ret_decay_f[layer], ret_decay_b[layer]])[:, None],
                               (2 * RET_HEADS, LANES)).astype(F32)
        wo = w_out[layer].astype(BF16)
        n2 = norm2_g[layer].reshape(1, d)
        rw = jnp.pad(router_w[layer], ((0, 0), (0, LANES - N_EXPERTS)))
        rwh = rw.astype(BF16)
        rw2 = jnp.concatenate([rwh, (rw - rwh.astype(F32)).astype(BF16)], axis=1)
        wg, wu, wd = (exp_w_gate[layer].astype(BF16), exp_w_up[layer].astype(BF16),
                      exp_w_down[layer].astype(BF16))

        res_x = None if layer == 0 else (acc_x, layer - 1)
        res_c = None if layer == 0 else (acc_c, layer - 1)
        x, ret_x, cqn_x, kcat_x, win_x = _inproj(x, mod, row_x, layer, g1, w_ext, qng, kvg,
                                                 (cw, sw, cm, sm), res_x, tm)
        ctx, ret_c, cqn_c, kcat_c, win_c = _inproj(ctx, mod, row_c, layer, g1, w_ext, qng, kvg,
                                                   None, res_c, min(tm, lc))
        ycf, ycb, yxf, yxb = _retention(ret_c, ret_x, dec, 4 if l % (4 * RET_CHUNK) == 0 else 1)
        mla_x, wn_x = _attention(cqn_x, wq_all, (cq, sq), jnp.concatenate([kcat_c, kcat_x], axis=1), wuv,
                                 win_x, win_x, win_c, win_sink[layer])
        x, h2x, afft_x = _outproj(x, yxf, yxb, ret_x, mla_x, wn_x, mod, row_x, layer, n2, wo, rw2, tm)
        idx_x = _route(afft_x, CAPACITY_FACTOR * l // N_EXPERTS)
        acc_x = _scatter(idx_x, afft_x, _expert_ffn(idx_x, h2x, wg, wu, wd, 1), 1)
        if need_ctx:
            mla_c, wn_c = _attention(cqn_c, wq_all, None, kcat_c, wuv, win_c, None, win_c, win_sink[layer])
            ctx, h2c, afft_c = _outproj(ctx, ycf, ycb, ret_c, mla_c, wn_c, mod, row_c, layer, n2, wo,
                                        rw2, min(tm, lc))
            idx_c = _route(afft_c, CAPACITY_FACTOR * lc // N_EXPERTS)
            acc_c = _scatter(idx_c, afft_c, _expert_ffn(idx_c, h2c, wg, wu, wd, b), b)
    return _final(x, acc_x, mod, depth - 1, final_g.reshape(1, d), tm)
```

```python
import functools

import numpy as np
import jax
import jax.numpy as jnp
from jax import lax
from jax.experimental import pallas as pl
from jax.experimental.pallas import tpu as pltpu

F32 = jnp.float32
BF16 = jnp.bfloat16

D_MODEL = 1024
DEPTH = 2
GRID_W = 64
RET_HEADS = 4
RET_DK = 64
RET_CHUNK = 128
MLA_HEADS = 8
MLA_Q_RANK = 256
MLA_KV_RANK = 128
MLA_NOPE = 64
MLA_ROPE = 32
MLA_DV = 64
WIN_Q_HEADS = 4
WIN_KV_HEADS = 2
WIN_DH = 64
WINDOW = 128
N_EXPERTS = 16
EXPERT_FF = 768
CAPACITY_FACTOR = 2
ROPE_BASE = 10000.0
NORM_EPS = 1e-6
GN_EPS = 1e-5
NEG_INF = -1e30
LOG2E = 1.4426950408889634

LANES = 128
MOD_ROWS = 16
ROUTE_BLOCKS = 32
KCAT = MLA_KV_RANK + MLA_ROPE
VMEM_LIMIT = 56 * 1024 * 1024

C_RET = 0
C_CQ = 1024
C_CKV = 1280
C_WQ = 1408
C_WQP = 1664
C_WK = 1920
C_WKP = 2048
C_WV = 2176
C_KR = 2304
N_EXT = 2432


def _cparams(sem):
    return pltpu.CompilerParams(dimension_semantics=sem, vmem_limit_bytes=VMEM_LIMIT)


def _dot(a, b):
    return jnp.dot(a, b, preferred_element_type=F32)


def _dot_nt(a, b):
    return lax.dot_general(a, b, (((1,), (1,)), ((), ())), preferred_element_type=F32)


def _dot_tn(a, b):
    return lax.dot_general(a, b, (((0,), (0,)), ((), ())), preferred_element_type=F32)


def _rms(x, g):
    return x * lax.rsqrt(jnp.mean(x * x, axis=-1, keepdims=True) + NORM_EPS) * g


def _silu(x):
    return x * jax.nn.sigmoid(x)


def _mod_kernel(c_ref, w_ref, b_ref, o_ref):
    o_ref[0] = jnp.dot(_silu(c_ref[...]), w_ref[0], preferred_element_type=F32,
                       precision=lax.Precision.HIGHEST) + b_ref[0]


def _modulation(c_all, ada_w, ada_b):
    depth, d, n = ada_w.shape
    tn = 1024
    return pl.pallas_call(
        _mod_kernel,
        out_shape=jax.ShapeDtypeStruct((depth, MOD_ROWS, n), F32),
        grid=(depth, n // tn),
        in_specs=[pl.BlockSpec((MOD_ROWS, d), lambda l, j: (0, 0)),
                  pl.BlockSpec((1, d, tn), lambda l, j: (l, 0, j)),
                  pl.BlockSpec((1, 1, tn), lambda l, j: (l, 0, j))],
        out_specs=pl.BlockSpec((1, MOD_ROWS, tn), lambda l, j: (l, 0, j)),
        compiler_params=_cparams(("arbitrary", "arbitrary")),
        name="adaln_mod",
    )(c_all, ada_w, ada_b.reshape(depth, 1, n))


def _wabs_kernel(uq_ref, uk_ref, o_ref):
    o_ref[0] = jnp.dot(uq_ref[0, 0], uk_ref[0, 0], preferred_element_type=F32,
                       precision=lax.Precision.HIGHEST)


def _absorbed_q_weights(w_uq, w_uk):
    depth = w_uq.shape[0]
    uq = jnp.transpose(w_uq[..., :MLA_NOPE], (0, 2, 1, 3))
    uk = jnp.transpose(w_uk, (0, 2, 3, 1))
    return pl.pallas_call(
        _wabs_kernel,
        out_shape=jax.ShapeDtypeStruct((depth, MLA_Q_RANK, MLA_HEADS * MLA_KV_RANK), F32),
        grid=(depth, MLA_HEADS),
        in_specs=[pl.BlockSpec((1, 1, MLA_Q_RANK, MLA_NOPE), lambda l, h: (l, h, 0, 0)),
                  pl.BlockSpec((1, 1, MLA_NOPE, MLA_KV_RANK), lambda l, h: (l, h, 0, 0))],
        out_specs=pl.BlockSpec((1, MLA_Q_RANK, MLA_KV_RANK), lambda l, h: (l, 0, h)),
        compiler_params=_cparams(("arbitrary", "arbitrary")),
        name="mla_absorb",
    )(uq, uk)


def _inproj_kernel(*refs, rope, resid):
    it = iter(refs)
    x_ref = next(it)
    if resid:
        acc_ref, g5_ref = next(it), next(it)
    g1_ref, sh_ref, sc_ref, w_ref, qng_ref, kvg_ref = (next(it) for _ in range(6))
    if rope:
        cw_ref, sw_ref, cm_ref, sm_ref = (next(it) for _ in range(4))
    if resid:
        xo_ref = next(it)
    ret_ref, cqn_ref, kcat_ref, win_ref = (next(it) for _ in range(4))

    x = x_ref[0]
    if resid:
        x = x + g5_ref[0] * acc_ref[0]
        xo_ref[0] = x
    h = _rms(x, g1_ref[...]) * (1.0 + sc_ref[0]) + sh_ref[0]
    p = _dot(h.astype(BF16), w_ref[...])

    ret_ref[0] = p[:, C_RET:C_CQ]
    cqn_ref[0] = _rms(p[:, C_CQ:C_CKV], qng_ref[...]).astype(BF16)
    kvn = _rms(p[:, C_CKV:C_WQ], kvg_ref[...])
    wq, wk, wv = p[:, C_WQ:C_WQP], p[:, C_WK:C_WKP], p[:, C_WV:C_KR]
    kr = p[:, C_KR:C_KR + MLA_ROPE]
    if rope:
        wqp, wkp = p[:, C_WQP:C_WK], p[:, C_WKP:C_WV]
        krp = p[:, C_KR + MLA_ROPE:C_KR + 2 * MLA_ROPE]
        cw, sw = cw_ref[...], sw_ref[...]
        wq = jnp.concatenate([wq[:, :LANES] * cw + wqp[:, :LANES] * sw,
                              wq[:, LANES:] * cw + wqp[:, LANES:] * sw], axis=1)
        wk = wk * cw + wkp * sw
        kr = kr * cm_ref[...] + krp * sm_ref[...]
    kcat_ref[0, :, 0:MLA_KV_RANK] = kvn.astype(BF16)
    kcat_ref[0, :, MLA_KV_RANK:KCAT] = kr.astype(BF16)
    win_ref[0, :, 0:256] = wq.astype(BF16)
    win_ref[0, :, 256:384] = wk.astype(BF16)
    win_ref[0, :, 384:512] = wv.astype(BF16)


def _inproj(x, mod, mrow, layer, g1, w_ext, qng, kvg, tables, resid, tm):
    b, l, d = x.shape
    rope = tables is not None

    def mspec(k, lay):
        return pl.BlockSpec((1, 1, d), lambda bi, i: ((lay * MOD_ROWS + mrow(bi)) * 6 + k, 0, 0))

    tok = lambda w: pl.BlockSpec((1, tm, w), lambda bi, i: (bi, i, 0))
    const = lambda a: pl.BlockSpec(a.shape, lambda bi, i: (0,) * a.ndim)
    args, specs = [x], [tok(d)]
    if resid is not None:
        args += [resid[0], mod]
        specs += [tok(d), mspec(5, resid[1])]
    args += [g1, mod, mod, w_ext, qng, kvg]
    specs += [const(g1), mspec(0, layer), mspec(1, layer), const(w_ext), const(qng), const(kvg)]
    if rope:
        for t in tables:
            args.append(t)
            specs.append(pl.BlockSpec((tm, t.shape[1]), lambda bi, i: (i, 0)))
    out_shape, out_specs = [], []
    if resid is not None:
        out_shape.append(jax.ShapeDtypeStruct((b, l, d), F32))
        out_specs.append(tok(d))
    out_shape += [jax.ShapeDtypeStruct((b, l, 1024), F32), jax.ShapeDtypeStruct((b, l, MLA_Q_RANK), BF16),
                  jax.ShapeDtypeStruct((b, l, KCAT), BF16), jax.ShapeDtypeStruct((b, l, 512), BF16)]
    out_specs += [tok(1024), tok(MLA_Q_RANK), tok(KCAT), tok(512)]
    outs = pl.pallas_call(
        functools.partial(_inproj_kernel, rope=rope, resid=resid is not None),
        out_shape=out_shape, grid=(b, l // tm), in_specs=specs, out_specs=out_specs,
        compiler_params=_cparams(("arbitrary", "arbitrary")),
        name="inproj",
    )(*args)
    if resid is None:
        return (x,) + tuple(outs)
    return tuple(outs)


def _ret_kernel(dec_ref, c_ref, xf_ref, xb_ref, ycf_ref, ycb_ref, yxf_ref, yxb_ref,
                s_ref, dm_ref, qw_ref, kw_ref, gc_ref):
    n = pl.program_id(1)
    c = RET_CHUNK
    dk = RET_DK

    @pl.when(n == 0)
    def _():
        s_ref[...] = jnp.zeros_like(s_ref)
        lg = jnp.log1p(-jnp.exp2(dec_ref[...]))
        ii = lax.broadcasted_iota(jnp.int32, (c, c), 0).astype(F32)
        jj = lax.broadcasted_iota(jnp.int32, (c, c), 1).astype(F32)
        ir = lax.broadcasted_iota(jnp.int32, (c, dk), 0).astype(F32)
        for h in range(RET_HEADS):
            lf = lg[h:h + 1, :]
            lb = lg[RET_HEADS + h:RET_HEADS + h + 1, :]
            keep_f = ii >= jj
            dm_ref[h] = jnp.where(keep_f, jnp.exp(lf * jnp.where(keep_f, ii - jj, 0.0)), 0.0)
            keep_b = jj > ii
            dm_ref[RET_HEADS + h] = jnp.where(keep_b, jnp.exp(lb * jnp.where(keep_b, jj - ii, 0.0)), 0.0)
            qw_ref[h] = jnp.exp(lf[:, :dk] * (ir + 1.0))
            kw_ref[h] = jnp.exp(lf[:, :dk] * (c - 1.0 - ir))
            qw_ref[RET_HEADS + h] = jnp.exp(lb[:, :dk] * (c - ir))
            kw_ref[RET_HEADS + h] = jnp.exp(lb[:, :dk] * ir)
            gc_ref[h] = jnp.exp(lf[:, :dk] * float(c))
            gc_ref[RET_HEADS + h] = jnp.exp(lb[:, :dk] * float(c))

    def run(src_ref, dst_ref, d):
        n_sub = src_ref.shape[1] // c
        order = range(n_sub) if d == 0 else range(n_sub - 1, -1, -1)
        for h in range(RET_HEADS):
            r = d * RET_HEADS + h
            intra, kv, qd = {}, {}, {}
            for g in order:
                rows = slice(g * c, (g + 1) * c)
                q = src_ref[0, rows, h * dk:(h + 1) * dk]
                k = src_ref[0, rows, 256 + h * dk:256 + (h + 1) * dk] * (RET_DK ** -0.5)
                v = src_ref[0, rows, 512 + h * dk:512 + (h + 1) * dk].astype(BF16)
                sc = _dot_nt(q.astype(BF16), k.astype(BF16)) * dm_ref[r]
                intra[g] = _dot(sc.astype(BF16), v)
                kv[g] = _dot_tn((k * kw_ref[r]).astype(BF16), v)
                qd[g] = (q * qw_ref[r]).astype(BF16)
            s = s_ref[r]
            for g in order:
                dst_ref[0, g * c:(g + 1) * c, h * dk:(h + 1) * dk] = intra[g] + _dot(qd[g], s.astype(BF16))
                s = gc_ref[r] * s + kv[g]
            s_ref[r] = s

    @pl.when(n == 0)
    def _():
        run(c_ref, ycf_ref, 0)
        run(c_ref, ycb_ref, 1)

    @pl.when(n > 0)
    def _():
        run(xf_ref, yxf_ref, 0)
        run(xb_ref, yxb_ref, 1)


def _retention(ret_c, ret_x, dec, g_sub):
    b, lc, _ = ret_c.shape
    lx = ret_x.shape[1]
    c = RET_CHUNK
    blk = g_sub * c
    assert lc % c == 0 and lx % blk == 0
    nx = lx // blk
    ci = lambda bi, n: (bi, 0, 0)
    xfi = lambda bi, n: (bi, jnp.maximum(n - 1, 0), 0)
    xbi = lambda bi, n: (bi, nx - 1 - jnp.maximum(n - 1, 0), 0)
    ydim = RET_HEADS * RET_DK
    return pl.pallas_call(
        _ret_kernel,
        out_shape=[jax.ShapeDtypeStruct((b, lc, ydim), F32), jax.ShapeDtypeStruct((b, lc, ydim), F32),
                   jax.ShapeDtypeStruct((b, lx, ydim), F32), jax.ShapeDtypeStruct((b, lx, ydim), F32)],
        grid=(b, 1 + nx),
        in_specs=[pl.BlockSpec(dec.shape, lambda bi, n: (0, 0)),
                  pl.BlockSpec((1, lc, 768), ci),
                  pl.BlockSpec((1, blk, 768), xfi), pl.BlockSpec((1, blk, 768), xbi)],
        out_specs=[pl.BlockSpec((1, lc, ydim), ci), pl.BlockSpec((1, lc, ydim), ci),
                   pl.BlockSpec((1, blk, ydim), xfi), pl.BlockSpec((1, blk, ydim), xbi)],
        scratch_shapes=[pltpu.VMEM((2 * RET_HEADS, RET_DK, RET_DK), F32),
                        pltpu.VMEM((2 * RET_HEADS, c, c), F32),
                        pltpu.VMEM((2 * RET_HEADS, c, RET_DK), F32),
                        pltpu.VMEM((2 * RET_HEADS, c, RET_DK), F32),
                        pltpu.VMEM((2 * RET_HEADS, 1, RET_DK), F32)],
        compiler_params=_cparams(("arbitrary", "arbitrary")),
        name="retention",
    )(dec, ret_c, ret_x, ret_x)


def _mla_kernel(*refs, rope, band, tq):
    it = iter(refs)
    cqn_ref, wq_ref = next(it), next(it)
    if rope:
        cq_ref, sq_ref = next(it), next(it)
    k_ref, wuv_ref, sink_ref, wnq_ref = (next(it) for _ in range(4))
    wkx_ref, wvx_ref = (next(it), next(it)) if band else (None, None)
    wkc_ref, wvc_ref, o_ref, wno_ref, qcat_ref, s_ref, m_ref = (next(it) for _ in range(7))

    nh = MLA_HEADS
    tk = k_ref.shape[1] // 2
    scale = (MLA_NOPE + MLA_ROPE) ** -0.5 * LOG2E
    qa = _dot(cqn_ref[0], wq_ref[...]) * scale
    qr = qa[:, nh * MLA_KV_RANK:nh * MLA_KV_RANK + nh * MLA_ROPE]
    if rope:
        qp = qa[:, nh * MLA_KV_RANK + nh * MLA_ROPE:]
        cq, sq = cq_ref[...], sq_ref[...]
        qr = jnp.concatenate([qr[:, :LANES] * cq + qp[:, :LANES] * sq,
                              qr[:, LANES:] * cq + qp[:, LANES:] * sq], axis=1)
    for h in range(nh):
        qcat_ref[h * tq:(h + 1) * tq, 0:MLA_KV_RANK] = qa[:, h * MLA_KV_RANK:(h + 1) * MLA_KV_RANK].astype(BF16)
        qcat_ref[h * tq:(h + 1) * tq, MLA_KV_RANK:KCAT] = qr[:, h * MLA_ROPE:(h + 1) * MLA_ROPE].astype(BF16)
    once = jnp.minimum(pl.program_id(1) + 1, 1)

    def run_once(fn):
        lax.fori_loop(0, once, lambda i, c: (fn(), c)[1], 0)

    def pass1():
        q = qcat_ref[...]
        mx = None
        for c in range(2):
            s = _dot_nt(q, k_ref[0, c * tk:(c + 1) * tk, :])
            s_ref[c] = s
            for j in range(0, tk, LANES):
                mx = s[:, j:j + LANES] if mx is None else jnp.maximum(mx, s[:, j:j + LANES])
        m_ref[...] = jnp.broadcast_to(jnp.max(mx, axis=-1, keepdims=True), m_ref.shape)
        _window_block(pl.program_id(1), sink_ref, wnq_ref, wkx_ref, wvx_ref, wkc_ref, wvc_ref, wno_ref)

    def pass2():
        m = m_ref[:, 0:1]
        ones_col = jnp.where(lax.broadcasted_iota(jnp.int32, (tk, LANES), 1) == 0, 1.0, 0.0).astype(BF16)

        def part(c):
            v_aug = jnp.concatenate([k_ref[0, c * tk:(c + 1) * tk, 0:MLA_KV_RANK], ones_col], axis=1)
            return _dot(jnp.exp2(s_ref[c] - m).astype(BF16), v_aug)

        acc = part(0) + part(1)
        o = (acc[:, 0:MLA_KV_RANK] * (1.0 / acc[:, MLA_KV_RANK:MLA_KV_RANK + 1])).astype(BF16)
        for h in range(nh):
            o_ref[0, :, h * MLA_DV:(h + 1) * MLA_DV] = _dot(o[h * tq:(h + 1) * tq, :], wuv_ref[h]).astype(BF16)

    run_once(pass1)
    run_once(pass2)


def _attention(cqn, wq_all, tables, keys, wuv, win_q, win_x, win_c, sink):
    b, l, _ = cqn.shape
    lk = keys.shape[1]
    lc = win_c.shape[1]
    tq = WINDOW
    rope = tables is not None
    band = win_x is not None
    kvw = WIN_KV_HEADS * WIN_DH
    assert lk % (2 * LANES) == 0 and l % tq == 0
    args = [cqn, wq_all]
    specs = [pl.BlockSpec((1, tq, MLA_Q_RANK), lambda bi, i: (bi, i, 0)),
             pl.BlockSpec(wq_all.shape, lambda bi, i: (0, 0))]
    if rope:
        for t in tables:
            args.append(t)
            specs.append(pl.BlockSpec((tq, LANES), lambda bi, i: (i, 0)))
    args.append(keys)
    specs.append(pl.BlockSpec((1, lk, KCAT), lambda bi, i: (bi, 0, 0)))
    args += [wuv, sink, win_q]
    specs += [pl.BlockSpec(wuv.shape, lambda bi, i: (0, 0, 0)), pl.BlockSpec(memory_space=pltpu.SMEM),
              pl.BlockSpec((1, tq, 256), lambda bi, i: (bi, i, 0))]
    if band:
        args += [win_x, win_x]
        specs += [pl.BlockSpec((1, l, kvw), lambda bi, i: (bi, 0, 2)),
                  pl.BlockSpec((1, l, kvw), lambda bi, i: (bi, 0, 3))]
    args += [win_c, win_c]
    specs += [pl.BlockSpec((1, lc, kvw), lambda bi, i: (bi, 0, 2)),
              pl.BlockSpec((1, lc, kvw), lambda bi, i: (bi, 0, 3))]
    rows = MLA_HEADS * tq
    scratch = [pltpu.VMEM((rows, KCAT), BF16), pltpu.VMEM((2, rows, lk // 2), F32),
               pltpu.VMEM((rows, LANES), F32)]
    return pl.pallas_call(
        functools.partial(_mla_kernel, rope=rope, band=band, tq=tq),
        out_shape=[jax.ShapeDtypeStruct((b, l, MLA_HEADS * MLA_DV), BF16),
                   jax.ShapeDtypeStruct((b, l, WIN_Q_HEADS * WIN_DH), BF16)],
        grid=(b, l // tq), in_specs=specs,
        out_specs=[pl.BlockSpec((1, tq, MLA_HEADS * MLA_DV), lambda bi, i: (bi, i, 0)),
                   pl.BlockSpec((1, tq, WIN_Q_HEADS * WIN_DH), lambda bi, i: (bi, i, 0))],
        scratch_shapes=scratch,
        compiler_params=_cparams(("arbitrary", "arbitrary")),
        name="attention",
    )(*args)


def _window_block(n, sink_ref, q_ref, kx_ref, vx_ref, kc_ref, vc_ref, o_ref):
    band = kx_ref is not None
    w = WINDOW
    d = WIN_DH
    g = WIN_Q_HEADS // WIN_KV_HEADS
    scale = d ** -0.5
    q = q_ref[0]
    rows = g * w
    row_id = lax.broadcasted_iota(jnp.int32, (rows, 1), 0)
    if band:
        l_x = kx_ref.shape[1]
        start = pl.multiple_of(jnp.clip((n - 1) * w, 0, l_x - 3 * w), w)
        qi = n * w + lax.broadcasted_iota(jnp.int32, (rows, 3 * w), 0) % w
        kj = start + lax.broadcasted_iota(jnp.int32, (rows, 3 * w), 1)
        keep = jnp.abs(kj - qi) <= WINDOW
        kb_all = kx_ref[0, pl.ds(start, 3 * w), :]
        vb_all = vx_ref[0, pl.ds(start, 3 * w), :]
    kc_all, vc_all = kc_ref[0], vc_ref[0]
    for j in range(WIN_KV_HEADS):
        q2 = jnp.concatenate([q[:, (g * j + t) * d:(g * j + t + 1) * d] for t in range(g)], axis=0)
        sk = jnp.zeros((rows, 1), F32)
        for t in range(g):
            sk = jnp.where(row_id // w == t, sink_ref[g * j + t], sk)
        s_ctx = _dot_nt(q2, kc_all[:, j * d:(j + 1) * d]) * scale
        m = jnp.maximum(jnp.max(s_ctx, axis=-1, keepdims=True), sk)
        if band:
            s_loc = _dot_nt(q2, kb_all[:, j * d:(j + 1) * d]) * scale
            s_loc = jnp.where(keep, s_loc, NEG_INF)
            m = jnp.maximum(m, jnp.max(s_loc, axis=-1, keepdims=True))
        p_ctx = jnp.exp(s_ctx - m)
        l = jnp.sum(p_ctx, axis=-1, keepdims=True) + jnp.exp(sk - m)
        o = _dot(p_ctx.astype(BF16), vc_all[:, j * d:(j + 1) * d])
        if band:
            p_loc = jnp.exp(s_loc - m)
            l = l + jnp.sum(p_loc, axis=-1, keepdims=True)
            o = o + _dot(p_loc.astype(BF16), vb_all[:, j * d:(j + 1) * d])
        o = (o * (1.0 / l)).astype(BF16)
        for t in range(g):
            o_ref[0, :, (g * j + t) * d:(g * j + t + 1) * d] = o[t * w:(t + 1) * w, :]


def _outproj_kernel(x_ref, yf_ref, yb_ref, rg_ref, mla_ref, win_ref, g2_ref, sh_ref, sc_ref,
                    n2_ref, wo_ref, rw_ref, xo_ref, h2_ref, afft_ref):
    tm = x_ref.shape[1]
    sub = 128 if tm % 128 == 0 else tm
    for r0 in range(0, tm, sub):
        rows = slice(r0, r0 + sub)
        y = yf_ref[0, rows, :] + yb_ref[0, rows, :]
        parts = []
        for h in range(RET_HEADS):
            yh = y[:, h * RET_DK:(h + 1) * RET_DK]
            mu = jnp.mean(yh, axis=-1, keepdims=True)
            var = jnp.mean(jnp.square(yh - mu), axis=-1, keepdims=True)
            parts.append((yh - mu) * lax.rsqrt(var + GN_EPS))
        ret = jnp.concatenate(parts, axis=1) * _silu(rg_ref[0, rows, :])
        proj = (_dot(ret.astype(BF16), wo_ref[0:256, :]) + _dot(mla_ref[0, rows, :], wo_ref[256:768, :])
                + _dot(win_ref[0, rows, :], wo_ref[768:1024, :]))
        x = x_ref[0, rows, :] + g2_ref[0] * proj
        xo_ref[0, rows, :] = x
        h2 = _rms(x, n2_ref[...]) * (1.0 + sc_ref[0]) + sh_ref[0]
        h2_ref[0, rows, :] = h2
        hi = h2.astype(BF16)
        lo = (h2 - hi.astype(F32)).astype(BF16)
        pp = _dot(jnp.concatenate([hi, lo], axis=0), rw_ref[...])
        logits = (pp[:sub, :LANES] + pp[:sub, LANES:]) + (pp[sub:, :LANES] + pp[sub:, LANES:])
        lt = logits.T[0:N_EXPERTS, :]
        et = jnp.exp(lt - jnp.max(lt, axis=0, keepdims=True))
        afft_ref[0, :, rows] = et / jnp.sum(et, axis=0, keepdims=True)


def _outproj(x, yf, yb, ret, mla, win, mod, mrow, layer, n2, w_out, rw2, tm):
    b, l, d = x.shape

    def mspec(k):
        return pl.BlockSpec((1, 1, d), lambda bi, i: ((layer * MOD_ROWS + mrow(bi)) * 6 + k, 0, 0))

    tok = lambda w: pl.BlockSpec((1, tm, w), lambda bi, i: (bi, i, 0))
    const = lambda a: pl.BlockSpec(a.shape, lambda bi, i: (0,) * a.ndim)
    return pl.pallas_call(
        _outproj_kernel,
        out_shape=[jax.ShapeDtypeStruct((b, l, d), F32), jax.ShapeDtypeStruct((b, l, d), F32),
                   jax.ShapeDtypeStruct((b, N_EXPERTS, l), F32)],
        grid=(b, l // tm),
        in_specs=[tok(d), tok(256), tok(256), pl.BlockSpec((1, tm, 256), lambda bi, i: (bi, i, 3)),
                  tok(512), tok(256), mspec(2), mspec(3), mspec(4), const(n2), const(w_out),
                  const(rw2)],
        out_specs=[tok(d), tok(d), pl.BlockSpec((1, N_EXPERTS, tm), lambda bi, i: (bi, 0, i))],
        compiler_params=_cparams(("arbitrary", "arbitrary")),
        name="outproj",
    )(x, yf, yb, ret, mla, win, mod, mod, mod, n2, w_out, rw2)


def _route_kernel(a_ref, tri_ref, bd_ref, idx_ref, place_ref, exc_ref, inc_ref, *, cap):
    a = a_ref[...]
    ne, nb, _ = a.shape
    grp = bd_ref.shape[0]
    as_f32 = lambda bits: lax.bitcast_convert_type(bits, F32)
    count = lambda m: jnp.sum(jnp.sum(m, axis=1, keepdims=True), axis=2, keepdims=True)

    def earlier_blocks(t):
        tb = t.astype(BF16)
        return jnp.concatenate([_dot(bd_ref[...], tb[r:r + grp, :]) for r in range(0, ne * nb, grp)], axis=0)

    def search(_, c):
        lo, hi = c
        mid = lo + ((hi - lo) >> 1)
        ok = count(jnp.where(a >= as_f32(mid), 1.0, 0.0)) >= cap
        return jnp.where(ok, mid, lo), jnp.where(ok, hi, mid)

    lo, hi = lax.fori_loop(0, 31, search, (jnp.zeros((ne, 1, 1), jnp.int32),
                                           jnp.full((ne, 1, 1), 0x7F800000, jnp.int32)))
    gt3 = jnp.where(a >= as_f32(hi), 1.0, 0.0)
    eq = (jnp.where(a >= as_f32(lo), 1.0, 0.0) - gt3).reshape(ne * nb, LANES)
    gt = gt3.reshape(ne * nb, LANES)
    need = jnp.broadcast_to(cap - count(gt3), a.shape).reshape(ne * nb, LANES)

    cum_eq = _dot(eq.astype(BF16), tri_ref[...])
    tot_eq = jnp.broadcast_to(cum_eq[:, LANES - 1:LANES], cum_eq.shape)
    before = earlier_blocks(tot_eq) + cum_eq - eq
    sel = gt + eq * jnp.where(before < need, 1.0, 0.0)
    selb = sel.astype(BF16)
    cum = _dot(selb, tri_ref[...])
    tot = jnp.broadcast_to(cum[:, LANES - 1:LANES], cum.shape)
    place = sel * (earlier_blocks(tot) + cum)
    high = jnp.floor(place * (1.0 / LANES))
    place_ref[:, 0:LANES] = high
    place_ref[:, LANES:2 * LANES] = place - high * LANES
    earlier = earlier_blocks(sel)
    exc_ref[...] = earlier
    inc_ref[...] = earlier + sel

    slot = lax.broadcasted_iota(jnp.int32, (cap, nb), 0).astype(F32)
    slot1 = (lax.broadcasted_iota(jnp.int32, (cap, LANES), 0) + 1).astype(F32)
    ones8 = jnp.ones((8, LANES), BF16)
    ids = jnp.concatenate([lax.broadcasted_iota(jnp.int32, (8, LANES), 1),
                           lax.broadcasted_iota(jnp.int32, (8, nb), 1) * LANES], axis=1).astype(BF16)

    def per_expert(e, _):
        rows = pl.ds(pl.multiple_of(e * nb, nb), nb)
        inc = _dot_nt(ones8, inc_ref[rows, :].astype(BF16))[0:1, :]
        exc = _dot_nt(ones8, exc_ref[rows, :].astype(BF16))[0:1, :]
        blk = jnp.where(slot >= exc, 1.0, 0.0) * jnp.where(slot < inc, 1.0, 0.0)
        got = _dot(blk.astype(BF16), place_ref[rows, :].astype(BF16))
        match = jnp.where(got[:, 0:LANES] * LANES + got[:, LANES:] == slot1, 1.0, 0.0)
        tok = _dot_nt(ids, jnp.concatenate([match, blk], axis=1).astype(BF16))
        idx_ref[pl.ds(e, 1), :, :] = tok[0:1, :].astype(jnp.int32).reshape(1, 1, cap)
        return 0

    lax.fori_loop(0, ne, per_expert, 0, unroll=4)


def _route(afft, cap):
    b, ne, l = afft.shape
    nb = max(l // LANES, ROUTE_BLOCKS)
    g = b * ne
    a3 = jnp.pad(afft, ((0, 0), (0, 0), (0, nb * LANES - l))).reshape(g, nb, LANES)
    u = np.arange(LANES)
    tri = jnp.asarray(u[:, None] <= u[None, :], BF16)
    r = np.arange(ne * nb)
    bd = jnp.asarray((r[:, None] // nb == r[None, :] // nb) & (r[None, :] % nb < r[:, None] % nb), BF16)
    const = lambda x: pl.BlockSpec(x.shape, lambda i: (0,) * x.ndim)
    idx = pl.pallas_call(
        functools.partial(_route_kernel, cap=cap),
        out_shape=jax.ShapeDtypeStruct((g, 1, cap), jnp.int32),
        grid=(1,),
        in_specs=[const(a3), const(tri), const(bd)],
        out_specs=pl.BlockSpec((g, 1, cap), lambda i: (0, 0, 0)),
        scratch_shapes=[pltpu.VMEM((g * nb, 2 * LANES), F32), pltpu.VMEM((g * nb, LANES), F32),
                        pltpu.VMEM((g * nb, LANES), F32)],
        compiler_params=_cparams(("arbitrary",)),
        name="route",
    )(a3, tri, bd)
    return idx.reshape(b, ne, 1, cap)


def _ffn_kernel(idx_ref, h_ref, wg_ref, wu_ref, wd_ref, ys_ref, xa_ref, xb_ref, *, nb, cap, ne):
    e = pl.program_id(1)

    def gather_rolled(ex, dst):
        for s in range(nb):
            def body(j, _):
                dst[pl.ds(s * cap + j, 1), :] = h_ref[s, pl.ds(idx_ref[s, ex, 0, j], 1), :]
                return 0
            lax.fori_loop(0, cap, body, 0, unroll=8)

    def gather_unrolled(ex, dst):
        for s in range(nb):
            for j in range(cap):
                dst[s * cap + j:s * cap + j + 1, :] = h_ref[s, pl.ds(idx_ref[s, ex, 0, j], 1), :]

    def ffn(src):
        xb = src[...].astype(BF16)
        hid = _silu(_dot(xb, wg_ref[0].astype(BF16))) * _dot(xb, wu_ref[0].astype(BF16))
        y = _dot(hid.astype(BF16), wd_ref[0].astype(BF16))
        for s in range(nb):
            ys_ref[s, 0] = y[s * cap:(s + 1) * cap, :]

    @pl.when(e == 0)
    def _():
        gather_rolled(0, xa_ref)

    e_next = jnp.minimum(e + 1, ne - 1)

    @pl.when(e % 2 == 0)
    def _():
        gather_unrolled(e_next, xb_ref)
        ffn(xa_ref)

    @pl.when(e % 2 == 1)
    def _():
        gather_unrolled(e_next, xa_ref)
        ffn(xb_ref)


def _expert_ffn(idx, h2, wg, wu, wd, layer, nb):
    b, l, d = h2.shape
    ne, cap = idx.shape[1], idx.shape[3]
    ff = wg.shape[2]
    assert ne % 2 == 0
    wsel = lambda bi, e: (layer * ne + e, 0, 0)
    return pl.pallas_call(
        functools.partial(_ffn_kernel, nb=nb, cap=cap, ne=ne),
        out_shape=jax.ShapeDtypeStruct((b, ne, cap, d), F32),
        grid=(b // nb, ne),
        in_specs=[pl.BlockSpec((nb, ne, 1, cap), lambda bi, e: (bi, 0, 0, 0), memory_space=pltpu.SMEM),
                  pl.BlockSpec((nb, l, d), lambda bi, e: (bi, 0, 0), pipeline_mode=pl.Buffered(1)),
                  pl.BlockSpec((1, d, ff), wsel), pl.BlockSpec((1, d, ff), wsel), pl.BlockSpec((1, ff, d), wsel)],
        out_specs=pl.BlockSpec((nb, 1, cap, d), lambda bi, e: (bi, e, 0, 0)),
        scratch_shapes=[pltpu.VMEM((nb * cap, d), F32), pltpu.VMEM((nb * cap, d), F32)],
        compiler_params=_cparams(("arbitrary", "arbitrary")),
        name="expert_ffn",
    )(idx, h2, wg, wu, wd)


def _scatter_kernel(idx_ref, aff_ref, ys_ref, acc_ref, *, nb, cap):
    @pl.when(pl.program_id(1) == 0)
    def _():
        acc_ref[...] = jnp.zeros_like(acc_ref)

    grp = 8
    for s in range(nb):
        def add(gi, _):
            j0 = gi * grp
            ts = [idx_ref[s, 0, 0, j0 + k] for k in range(grp)]
            new = [acc_ref[s, pl.ds(ts[k], 1), :]
                   + ys_ref[s, 0, pl.ds(j0 + k, 1), :] * aff_ref[s, 0, 0, ts[k]] for k in range(grp)]
            for k in range(grp):
                acc_ref[s, pl.ds(ts[k], 1), :] = new[k]
            return 0
        lax.fori_loop(0, cap // grp, add, 0)


def _scatter(idx, afft, ys, nb):
    b, ne, cap, d = ys.shape
    l = afft.shape[2]
    smem = lambda w: pl.BlockSpec((nb, 1, 1, w), lambda bi, e: (bi, e, 0, 0), memory_space=pltpu.SMEM)
    return pl.pallas_call(
        functools.partial(_scatter_kernel, nb=nb, cap=cap),
        out_shape=jax.ShapeDtypeStruct((b, l, d), F32),
        grid=(b // nb, ne),
        in_specs=[smem(cap), smem(l), pl.BlockSpec((nb, 1, cap, d), lambda bi, e: (bi, e, 0, 0))],
        out_specs=pl.BlockSpec((nb, l, d), lambda bi, e: (bi, 0, 0)),
        compiler_params=_cparams(("arbitrary", "arbitrary")),
        name="moe_scatter",
    )(idx, afft.reshape(b, ne, 1, l), ys)


def _final_kernel(x_ref, acc_ref, g5_ref, fg_ref, o_ref):
    o_ref[0] = _rms(x_ref[0] + g5_ref[0] * acc_ref[0], fg_ref[...])


def _final(x, acc, mod, layer, fg, tm):
    b, l, d = x.shape
    tok = pl.BlockSpec((1, tm, d), lambda bi, i: (bi, i, 0))
    return pl.pallas_call(
        _final_kernel,
        out_shape=jax.ShapeDtypeStruct((b, l, d), F32),
        grid=(b, l // tm),
        in_specs=[tok, tok, pl.BlockSpec((1, 1, d), lambda bi, i: ((layer * MOD_ROWS + bi) * 6 + 5, 0, 0)),
                  pl.BlockSpec(fg.shape, lambda bi, i: (0, 0))],
        out_specs=tok,
        compiler_params=_cparams(("arbitrary", "arbitrary")),
        name="final_norm",
    )(x, acc, mod, fg)


def _partner(dh):
    q = dh // 4
    return np.concatenate([np.arange(q, 2 * q), np.arange(0, q), np.arange(3 * q, 4 * q), np.arange(2 * q, 3 * q)])


def _rope_tables(n_tok, dh, reps):
    rows = n_tok // GRID_W
    row = jnp.repeat(jnp.arange(rows, dtype=jnp.int32), GRID_W)
    col = jnp.tile(jnp.arange(GRID_W, dtype=jnp.int32), rows)
    half = dh // 2
    inv = ROPE_BASE ** (-jnp.arange(0, half, 2, dtype=jnp.float32) / half)
    ar = row.astype(jnp.float32)[:, None] * inv[None, :]
    ac = col.astype(jnp.float32)[:, None] * inv[None, :]
    cos = jnp.concatenate([jnp.cos(ar), jnp.cos(ar), jnp.cos(ac), jnp.cos(ac)], axis=1)
    sin = jnp.concatenate([-jnp.sin(ar), jnp.sin(ar), -jnp.sin(ac), jnp.sin(ac)], axis=1)
    return jnp.tile(cos, (1, reps)), jnp.tile(sin, (1, reps))


def _extend_w_in(w):
    d = w.shape[0]
    wq, wk, wv = w[:, 1440:1696], w[:, 1696:1824], w[:, 1824:1952]
    kr = w[:, 1408:1440]
    pq = np.concatenate([h * WIN_DH + _partner(WIN_DH) for h in range(WIN_Q_HEADS)])
    pk = np.concatenate([h * WIN_DH + _partner(WIN_DH) for h in range(WIN_KV_HEADS)])
    ext = jnp.concatenate([w[:, 0:1408], wq, wq[:, pq], wk, wk[:, pk], wv, kr, kr[:, _partner(MLA_ROPE)],
                           jnp.zeros((d, N_EXT - C_KR - 2 * MLA_ROPE), w.dtype)], axis=1)
    return ext.astype(BF16)


def kernel(x, c, ctx, c_ctx, norm1_g, norm2_g, ada_w, ada_b, w_in, ret_decay_f, ret_decay_b,
           mla_qnorm_g, mla_kvnorm_g, mla_w_uq, mla_w_uk, mla_w_uv, win_sink, w_out,
           router_w, exp_w_gate, exp_w_up, exp_w_down, final_g):
    b, l, d = x.shape
    lc = ctx.shape[1]
    depth = w_in.shape[0]
    assert b + 1 <= MOD_ROWS and l % 512 == 0 and lc % RET_CHUNK == 0 and l >= 3 * WINDOW

    c_all = jnp.concatenate([c, c_ctx[None, :], jnp.zeros((MOD_ROWS - b - 1, d), F32)], axis=0)
    mod = _modulation(c_all, ada_w, ada_b).reshape(depth * MOD_ROWS * 6, 1, d)
    row_x = lambda bi: bi
    row_c = lambda bi: b

    wabs = _absorbed_q_weights(mla_w_uq, mla_w_uk)
    cw, sw = _rope_tables(l, WIN_DH, LANES // WIN_DH)
    cm, sm = _rope_tables(l, MLA_ROPE, 1)
    cq, sq = _rope_tables(l, MLA_ROPE, LANES // MLA_ROPE)
    pr = np.concatenate([h * MLA_ROPE + _partner(MLA_ROPE) for h in range(MLA_HEADS)])

    tm = 1024 if l % 1024 == 0 else 512
    acc_x = acc_c = None
    for layer in range(depth):
        need_ctx = layer < depth - 1
        w_ext = _extend_w_in(w_in[layer])
        g1 = norm1_g[layer].reshape(1, d)
        qng = mla_qnorm_g[layer].reshape(1, -1)
        kvg = mla_kvnorm_g[layer].reshape(1, -1)
        uq_rope = mla_w_uq[layer][:, :, MLA_NOPE:].reshape(MLA_Q_RANK, MLA_HEADS * MLA_ROPE)
        wq_all = jnp.concatenate([wabs[layer], uq_rope, uq_rope[:, pr]], axis=1).astype(BF16)
        wuv = jnp.transpose(mla_w_uv[layer], (1, 0, 2)).astype(BF16)
        dec = jnp.broadcast_to(jnp.concatenate([ret_decay_f[layer], ret_decay_b[layer]])[:, None],
                               (2 * RET_HEADS, LANES)).astype(F32)
        wo = w_out[layer].astype(BF16)
        n2 = norm2_g[layer].reshape(1, d)
        rw = jnp.pad(router_w[layer], ((0, 0), (0, LANES - N_EXPERTS)))
        rwh = rw.astype(BF16)
        rw2 = jnp.concatenate([rwh, (rw - rwh.astype(F32)).astype(BF16)], axis=1)
        wg, wu, wd = (w.reshape((-1,) + w.shape[2:]) for w in (exp_w_gate, exp_w_up, exp_w_down))

        res_x = None if layer == 0 else (acc_x, layer - 1)
        res_c = None if layer == 0 else (acc_c, layer - 1)
        x, ret_x, cqn_x, kcat_x, win_x = _inproj(x, mod, row_x, layer, g1, w_ext, qng, kvg,
                                                 (cw, sw, cm, sm), res_x, tm)
        ctx, ret_c, cqn_c, kcat_c, win_c = _inproj(ctx, mod, row_c, layer, g1, w_ext, qng, kvg,
                                                   None, res_c, min(tm, lc))
        ycf, ycb, yxf, yxb = _retention(ret_c, ret_x, dec, 4 if l % (4 * RET_CHUNK) == 0 else 1)
        mla_x, wn_x = _attention(cqn_x, wq_all, (cq, sq), jnp.concatenate([kcat_c, kcat_x], axis=1), wuv,
                                 win_x, win_x, win_c, win_sink[layer])
        x, h2x, afft_x = _outproj(x, yxf, yxb, ret_x, mla_x, wn_x, mod, row_x, layer, n2, wo, rw2, tm)
        idx_x = _route(afft_x, CAPACITY_FACTOR * l // N_EXPERTS)
        acc_x = _scatter(idx_x, afft_x, _expert_ffn(idx_x, h2x, wg, wu, wd, layer, 1), 1)
        if need_ctx:
            mla_c, wn_c = _attention(cqn_c, wq_all, None, kcat_c, wuv, win_c, None, win_c, win_sink[layer])
            ctx, h2c, afft_c = _outproj(ctx, ycf, ycb, ret_c, mla_c, wn_c, mod, row_c, layer, n2, wo,
                                        rw2, min(tm, lc))
            idx_c = _route(afft_c, CAPACITY_FACTOR * lc // N_EXPERTS)
            acc_c = _scatter(idx_c, afft_c, _expert_ffn(idx_c, h2c, wg, wu, wd, layer, b), b)
    return _final(x, acc_x, mod, depth - 1, final_g.reshape(1, d), tm)
```

```python
import functools

import numpy as np
import jax
import jax.numpy as jnp
from jax import lax
from jax.experimental import pallas as pl
from jax.experimental.pallas import tpu as pltpu

F32 = jnp.float32
BF16 = jnp.bfloat16

D_MODEL = 1024
DEPTH = 2
GRID_W = 64
RET_HEADS = 4
RET_DK = 64
RET_CHUNK = 128
MLA_HEADS = 8
MLA_Q_RANK = 256
MLA_KV_RANK = 128
MLA_NOPE = 64
MLA_ROPE = 32
MLA_DV = 64
WIN_Q_HEADS = 4
WIN_KV_HEADS = 2
WIN_DH = 64
WINDOW = 128
N_EXPERTS = 16
EXPERT_FF = 768
CAPACITY_FACTOR = 2
ROPE_BASE = 10000.0
NORM_EPS = 1e-6
GN_EPS = 1e-5
NEG_INF = -1e30
LOG2E = 1.4426950408889634

LANES = 128
MOD_ROWS = 16
ROUTE_BLOCKS = 32
KCAT = MLA_KV_RANK + MLA_ROPE
VMEM_LIMIT = 56 * 1024 * 1024

C_RET = 0
C_CQ = 1024
C_CKV = 1280
C_WQ = 1408
C_WQP = 1664
C_WK = 1920
C_WKP = 2048
C_WV = 2176
C_KR = 2304
N_EXT = 2432


def _cparams(sem):
    return pltpu.CompilerParams(dimension_semantics=sem, vmem_limit_bytes=VMEM_LIMIT)


def _dot(a, b):
    return jnp.dot(a, b, preferred_element_type=F32)


def _dot_nt(a, b):
    return lax.dot_general(a, b, (((1,), (1,)), ((), ())), preferred_element_type=F32)


def _dot_tn(a, b):
    return lax.dot_general(a, b, (((0,), (0,)), ((), ())), preferred_element_type=F32)


def _rms(x, g):
    return x * lax.rsqrt(jnp.mean(x * x, axis=-1, keepdims=True) + NORM_EPS) * g


def _silu(x):
    return x * jax.nn.sigmoid(x)


def _mod_kernel(c_ref, w_ref, b_ref, o_ref):
    o_ref[0] = jnp.dot(_silu(c_ref[...]), w_ref[0], preferred_element_type=F32,
                       precision=lax.Precision.HIGHEST) + b_ref[0]


def _modulation(c_all, ada_w, ada_b):
    depth, d, n = ada_w.shape
    tn = 1024
    return pl.pallas_call(
        _mod_kernel,
        out_shape=jax.ShapeDtypeStruct((depth, MOD_ROWS, n), F32),
        grid=(depth, n // tn),
        in_specs=[pl.BlockSpec((MOD_ROWS, d), lambda l, j: (0, 0)),
                  pl.BlockSpec((1, d, tn), lambda l, j: (l, 0, j)),
                  pl.BlockSpec((1, 1, tn), lambda l, j: (l, 0, j))],
        out_specs=pl.BlockSpec((1, MOD_ROWS, tn), lambda l, j: (l, 0, j)),
        compiler_params=_cparams(("arbitrary", "arbitrary")),
        name="adaln_mod",
    )(c_all, ada_w, ada_b.reshape(depth, 1, n))


def _wabs_kernel(uq_ref, uk_ref, o_ref):
    o_ref[0] = jnp.dot(uq_ref[0, 0], uk_ref[0, 0], preferred_element_type=F32,
                       precision=lax.Precision.HIGHEST)


def _absorbed_q_weights(w_uq, w_uk):
    depth = w_uq.shape[0]
    uq = jnp.transpose(w_uq[..., :MLA_NOPE], (0, 2, 1, 3))
    uk = jnp.transpose(w_uk, (0, 2, 3, 1))
    return pl.pallas_call(
        _wabs_kernel,
        out_shape=jax.ShapeDtypeStruct((depth, MLA_Q_RANK, MLA_HEADS * MLA_KV_RANK), F32),
        grid=(depth, MLA_HEADS),
        in_specs=[pl.BlockSpec((1, 1, MLA_Q_RANK, MLA_NOPE), lambda l, h: (l, h, 0, 0)),
                  pl.BlockSpec((1, 1, MLA_NOPE, MLA_KV_RANK), lambda l, h: (l, h, 0, 0))],
        out_specs=pl.BlockSpec((1, MLA_Q_RANK, MLA_KV_RANK), lambda l, h: (l, 0, h)),
        compiler_params=_cparams(("arbitrary", "arbitrary")),
        name="mla_absorb",
    )(uq, uk)


def _inproj_kernel(*refs, rope, resid):
    it = iter(refs)
    x_ref = next(it)
    if resid:
        acc_ref, g5_ref = next(it), next(it)
    g1_ref, sh_ref, sc_ref, w_ref, qng_ref, kvg_ref = (next(it) for _ in range(6))
    if rope:
        cw_ref, sw_ref, cm_ref, sm_ref = (next(it) for _ in range(4))
    if resid:
        xo_ref = next(it)
    ret_ref, cqn_ref, kcat_ref, win_ref = (next(it) for _ in range(4))

    x = x_ref[0]
    if resid:
        x = x + g5_ref[0] * acc_ref[0]
        xo_ref[0] = x
    h = _rms(x, g1_ref[...]) * (1.0 + sc_ref[0]) + sh_ref[0]
    p = _dot(h.astype(BF16), w_ref[...])

    ret_ref[0] = p[:, C_RET:C_CQ]
    cqn_ref[0] = _rms(p[:, C_CQ:C_CKV], qng_ref[...]).astype(BF16)
    kvn = _rms(p[:, C_CKV:C_WQ], kvg_ref[...])
    wq, wk, wv = p[:, C_WQ:C_WQP], p[:, C_WK:C_WKP], p[:, C_WV:C_KR]
    kr = p[:, C_KR:C_KR + MLA_ROPE]
    if rope:
        wqp, wkp = p[:, C_WQP:C_WK], p[:, C_WKP:C_WV]
        krp = p[:, C_KR + MLA_ROPE:C_KR + 2 * MLA_ROPE]
        cw, sw = cw_ref[...], sw_ref[...]
        wq = jnp.concatenate([wq[:, :LANES] * cw + wqp[:, :LANES] * sw,
                              wq[:, LANES:] * cw + wqp[:, LANES:] * sw], axis=1)
        wk = wk * cw + wkp * sw
        kr = kr * cm_ref[...] + krp * sm_ref[...]
    kcat_ref[0, :, 0:MLA_KV_RANK] = kvn.astype(BF16)
    kcat_ref[0, :, MLA_KV_RANK:KCAT] = kr.astype(BF16)
    win_ref[0, :, 0:256] = wq.astype(BF16)
    win_ref[0, :, 256:384] = wk.astype(BF16)
    win_ref[0, :, 384:512] = wv.astype(BF16)


def _inproj(x, mod, mrow, layer, g1, w_ext, qng, kvg, tables, resid, tm):
    b, l, d = x.shape
    rope = tables is not None

    def mspec(k, lay):
        return pl.BlockSpec((1, 1, d), lambda bi, i: ((lay * MOD_ROWS + mrow(bi)) * 6 + k, 0, 0))

    tok = lambda w: pl.BlockSpec((1, tm, w), lambda bi, i: (bi, i, 0))
    const = lambda a: pl.BlockSpec(a.shape, lambda bi, i: (0,) * a.ndim)
    args, specs = [x], [tok(d)]
    if resid is not None:
        args += [resid[0], mod]
        specs += [tok(d), mspec(5, resid[1])]
    args += [g1, mod, mod, w_ext, qng, kvg]
    specs += [const(g1), mspec(0, layer), mspec(1, layer), const(w_ext), const(qng), const(kvg)]
    if rope:
        for t in tables:
            args.append(t)
            specs.append(pl.BlockSpec((tm, t.shape[1]), lambda bi, i: (i, 0)))
    out_shape, out_specs = [], []
    if resid is not None:
        out_shape.append(jax.ShapeDtypeStruct((b, l, d), F32))
        out_specs.append(tok(d))
    out_shape += [jax.ShapeDtypeStruct((b, l, 1024), F32), jax.ShapeDtypeStruct((b, l, MLA_Q_RANK), BF16),
                  jax.ShapeDtypeStruct((b, l, KCAT), BF16), jax.ShapeDtypeStruct((b, l, 512), BF16)]
    out_specs += [tok(1024), tok(MLA_Q_RANK), tok(KCAT), tok(512)]
    outs = pl.pallas_call(
        functools.partial(_inproj_kernel, rope=rope, resid=resid is not None),
        out_shape=out_shape, grid=(b, l // tm), in_specs=specs, out_specs=out_specs,
        compiler_params=_cparams(("arbitrary", "arbitrary")),
        name="inproj",
    )(*args)
    if resid is None:
        return (x,) + tuple(outs)
    return tuple(outs)


def _ret_kernel(dec_ref, c_ref, xf_ref, xb_ref, ycf_ref, ycb_ref, yxf_ref, yxb_ref,
                s_ref, dm_ref, qw_ref, kw_ref, gs_ref):
    n = pl.program_id(1)
    c = RET_CHUNK
    dk = RET_DK
    nh = RET_HEADS
    w = nh * dk
    lane_head = lax.broadcasted_iota(jnp.int32, (1, w), 1) // dk
    same_head = (lax.broadcasted_iota(jnp.int32, (w, w), 0) // dk) == (lax.broadcasted_iota(jnp.int32, (w, w), 1) // dk)

    @pl.when(n == 0)
    def _():
        s_ref[...] = jnp.zeros_like(s_ref)
        lg = jnp.log1p(-jnp.exp2(dec_ref[...]))
        ii = lax.broadcasted_iota(jnp.int32, (c, c), 0).astype(F32)
        jj = lax.broadcasted_iota(jnp.int32, (c, c), 1).astype(F32)
        ir = lax.broadcasted_iota(jnp.int32, (c, w), 0).astype(F32)
        row_head = lax.broadcasted_iota(jnp.int32, (w, w), 0) // dk
        for d in range(2):
            lrow = jnp.concatenate([lg[d * nh + h:d * nh + h + 1, 0:dk] for h in range(nh)], axis=1)
            qw_ref[d] = jnp.exp(lrow * (ir + 1.0)) if d == 0 else jnp.exp(lrow * (c - ir))
            kw_ref[d] = jnp.exp(lrow * (c - 1.0 - ir)) if d == 0 else jnp.exp(lrow * ir)
            gs = jnp.zeros((w, w), F32)
            for h in range(nh):
                lh = lg[d * nh + h:d * nh + h + 1, :]
                gs = jnp.where(row_head == h, jnp.exp(jnp.concatenate([lh, lh], axis=1) * float(c)), gs)
                keep = (ii >= jj) if d == 0 else (jj > ii)
                dist = jnp.where(keep, jnp.abs(ii - jj), 0.0)
                dm_ref[d * nh + h] = jnp.where(keep, jnp.exp(lh * dist), 0.0)
            gs_ref[d] = gs

    def run(src_ref, dst_ref, d):
        n_sub = src_ref.shape[1] // c
        order = range(n_sub) if d == 0 else range(n_sub - 1, -1, -1)
        intra, kv, qd = {}, {}, {}
        for g in order:
            rows = slice(g * c, (g + 1) * c)
            q = src_ref[0, rows, 0:w]
            k = src_ref[0, rows, w:2 * w] * (RET_DK ** -0.5)
            v = src_ref[0, rows, 2 * w:3 * w]
            kb, vb = k.astype(BF16), v.astype(BF16)
            scs = [(_dot_nt(jnp.where(lane_head == h, q, 0.0).astype(BF16), kb) * dm_ref[d * nh + h]).astype(BF16)
                   for h in range(nh)]
            v_heads = jnp.concatenate([jnp.where(lane_head == h, v, 0.0).astype(BF16) for h in range(nh)], axis=0)
            intra[g] = _dot(jnp.concatenate(scs, axis=1), v_heads)
            kv[g] = jnp.where(same_head, _dot_tn((k * kw_ref[d]).astype(BF16), vb), 0.0)
            qd[g] = (q * qw_ref[d]).astype(BF16)
        s = s_ref[d]
        for g in order:
            dst_ref[0, g * c:(g + 1) * c, :] = intra[g] + _dot(qd[g], s.astype(BF16))
            s = gs_ref[d] * s + kv[g]
        s_ref[d] = s

    @pl.when(n == 0)
    def _():
        run(c_ref, ycf_ref, 0)
        run(c_ref, ycb_ref, 1)

    @pl.when(n > 0)
    def _():
        run(xf_ref, yxf_ref, 0)
        run(xb_ref, yxb_ref, 1)


def _retention(ret_c, ret_x, dec, g_sub):
    b, lc, _ = ret_c.shape
    lx = ret_x.shape[1]
    c = RET_CHUNK
    blk = g_sub * c
    assert lc % c == 0 and lx % blk == 0
    nx = lx // blk
    ci = lambda bi, n: (bi, 0, 0)
    xfi = lambda bi, n: (bi, jnp.maximum(n - 1, 0), 0)
    xbi = lambda bi, n: (bi, nx - 1 - jnp.maximum(n - 1, 0), 0)
    ydim = RET_HEADS * RET_DK
    return pl.pallas_call(
        _ret_kernel,
        out_shape=[jax.ShapeDtypeStruct((b, lc, ydim), F32), jax.ShapeDtypeStruct((b, lc, ydim), F32),
                   jax.ShapeDtypeStruct((b, lx, ydim), F32), jax.ShapeDtypeStruct((b, lx, ydim), F32)],
        grid=(b, 1 + nx),
        in_specs=[pl.BlockSpec(dec.shape, lambda bi, n: (0, 0)),
                  pl.BlockSpec((1, lc, 768), ci),
                  pl.BlockSpec((1, blk, 768), xfi), pl.BlockSpec((1, blk, 768), xbi)],
        out_specs=[pl.BlockSpec((1, lc, ydim), ci), pl.BlockSpec((1, lc, ydim), ci),
                   pl.BlockSpec((1, blk, ydim), xfi), pl.BlockSpec((1, blk, ydim), xbi)],
        scratch_shapes=[pltpu.VMEM((2, ydim, ydim), F32),
                        pltpu.VMEM((2 * RET_HEADS, c, c), F32),
                        pltpu.VMEM((2, c, ydim), F32),
                        pltpu.VMEM((2, c, ydim), F32),
                        pltpu.VMEM((2, ydim, ydim), F32)],
        compiler_params=_cparams(("arbitrary", "arbitrary")),
        name="retention",
    )(dec, ret_c, ret_x, ret_x)


def _mla_kernel(*refs, rope, band, tq):
    it = iter(refs)
    cqn_ref, wq_ref = next(it), next(it)
    if rope:
        cq_ref, sq_ref = next(it), next(it)
    k_ref, wuv_ref, sink_ref, wnq_ref = (next(it) for _ in range(4))
    wkx_ref, wvx_ref = (next(it), next(it)) if band else (None, None)
    wkc_ref, wvc_ref, o_ref, wno_ref, qcat_ref, s_ref, m_ref = (next(it) for _ in range(7))

    nh = MLA_HEADS
    tk = k_ref.shape[1] // 2
    scale = (MLA_NOPE + MLA_ROPE) ** -0.5 * LOG2E
    qa = _dot(cqn_ref[0], wq_ref[...]) * scale
    qr = qa[:, nh * MLA_KV_RANK:nh * MLA_KV_RANK + nh * MLA_ROPE]
    if rope:
        qp = qa[:, nh * MLA_KV_RANK + nh * MLA_ROPE:]
        cq, sq = cq_ref[...], sq_ref[...]
        qr = jnp.concatenate([qr[:, :LANES] * cq + qp[:, :LANES] * sq,
                              qr[:, LANES:] * cq + qp[:, LANES:] * sq], axis=1)
    for h in range(nh):
        qcat_ref[h * tq:(h + 1) * tq, 0:MLA_KV_RANK] = qa[:, h * MLA_KV_RANK:(h + 1) * MLA_KV_RANK].astype(BF16)
        qcat_ref[h * tq:(h + 1) * tq, MLA_KV_RANK:KCAT] = qr[:, h * MLA_ROPE:(h + 1) * MLA_ROPE].astype(BF16)
    once = jnp.minimum(pl.program_id(1) + 1, 1)

    def run_once(fn):
        lax.fori_loop(0, once, lambda i, c: (fn(), c)[1], 0)

    def pass1():
        q = qcat_ref[...]
        mx = None
        for c in range(2):
            s = _dot_nt(q, k_ref[0, c * tk:(c + 1) * tk, :])
            s_ref[c] = s
            for j in range(0, tk, LANES):
                mx = s[:, j:j + LANES] if mx is None else jnp.maximum(mx, s[:, j:j + LANES])
        m_ref[...] = jnp.broadcast_to(jnp.max(mx, axis=-1, keepdims=True), m_ref.shape)
        _window_block(pl.program_id(1), sink_ref, wnq_ref, wkx_ref, wvx_ref, wkc_ref, wvc_ref, wno_ref)

    def pass2():
        m = m_ref[:, 0:1]
        ones_col = jnp.where(lax.broadcasted_iota(jnp.int32, (tk, LANES), 1) == 0, 1.0, 0.0).astype(BF16)

        def part(c):
            v_aug = jnp.concatenate([k_ref[0, c * tk:(c + 1) * tk, 0:MLA_KV_RANK], ones_col], axis=1)
            return _dot(jnp.exp2(s_ref[c] - m).astype(BF16), v_aug)

        acc = part(0) + part(1)
        o = (acc[:, 0:MLA_KV_RANK] * (1.0 / acc[:, MLA_KV_RANK:MLA_KV_RANK + 1])).astype(BF16)
        for h in range(nh):
            o_ref[0, :, h * MLA_DV:(h + 1) * MLA_DV] = _dot(o[h * tq:(h + 1) * tq, :], wuv_ref[h]).astype(BF16)

    run_once(pass1)
    run_once(pass2)


def _attention(cqn, wq_all, tables, keys, wuv, win_q, win_x, win_c, sink):
    b, l, _ = cqn.shape
    lk = keys.shape[1]
    lc = win_c.shape[1]
    tq = WINDOW
    rope = tables is not None
    band = win_x is not None
    kvw = WIN_KV_HEADS * WIN_DH
    assert lk % (2 * LANES) == 0 and l % tq == 0
    args = [cqn, wq_all]
    specs = [pl.BlockSpec((1, tq, MLA_Q_RANK), lambda bi, i: (bi, i, 0)),
             pl.BlockSpec(wq_all.shape, lambda bi, i: (0, 0))]
    if rope:
        for t in tables:
            args.append(t)
            specs.append(pl.BlockSpec((tq, LANES), lambda bi, i: (i, 0)))
    args.append(keys)
    specs.append(pl.BlockSpec((1, lk, KCAT), lambda bi, i: (bi, 0, 0)))
    args += [wuv, sink, win_q]
    specs += [pl.BlockSpec(wuv.shape, lambda bi, i: (0, 0, 0)), pl.BlockSpec(memory_space=pltpu.SMEM),
              pl.BlockSpec((1, tq, 256), lambda bi, i: (bi, i, 0))]
    if band:
        args += [win_x, win_x]
        specs += [pl.BlockSpec((1, l, kvw), lambda bi, i: (bi, 0, 2)),
                  pl.BlockSpec((1, l, kvw), lambda bi, i: (bi, 0, 3))]
    args += [win_c, win_c]
    specs += [pl.BlockSpec((1, lc, kvw), lambda bi, i: (bi, 0, 2)),
              pl.BlockSpec((1, lc, kvw), lambda bi, i: (bi, 0, 3))]
    rows = MLA_HEADS * tq
    scratch = [pltpu.VMEM((rows, KCAT), BF16), pltpu.VMEM((2, rows, lk // 2), F32),
               pltpu.VMEM((rows, LANES), F32)]
    return pl.pallas_call(
        functools.partial(_mla_kernel, rope=rope, band=band, tq=tq),
        out_shape=[jax.ShapeDtypeStruct((b, l, MLA_HEADS * MLA_DV), BF16),
                   jax.ShapeDtypeStruct((b, l, WIN_Q_HEADS * WIN_DH), BF16)],
        grid=(b, l // tq), in_specs=specs,
        out_specs=[pl.BlockSpec((1, tq, MLA_HEADS * MLA_DV), lambda bi, i: (bi, i, 0)),
                   pl.BlockSpec((1, tq, WIN_Q_HEADS * WIN_DH), lambda bi, i: (bi, i, 0))],
        scratch_shapes=scratch,
        compiler_params=_cparams(("arbitrary", "arbitrary")),
        name="attention",
    )(*args)


def _window_block(n, sink_ref, q_ref, kx_ref, vx_ref, kc_ref, vc_ref, o_ref):
    band = kx_ref is not None
    w = WINDOW
    d = WIN_DH
    g = WIN_Q_HEADS // WIN_KV_HEADS
    scale = d ** -0.5
    q = q_ref[0]
    rows = g * w
    row_id = lax.broadcasted_iota(jnp.int32, (rows, 1), 0)
    if band:
        l_x = kx_ref.shape[1]
        start = pl.multiple_of(jnp.clip((n - 1) * w, 0, l_x - 3 * w), w)
        qi = n * w + lax.broadcasted_iota(jnp.int32, (rows, 3 * w), 0) % w
        kj = start + lax.broadcasted_iota(jnp.int32, (rows, 3 * w), 1)
        keep = jnp.abs(kj - qi) <= WINDOW
        kb_all = kx_ref[0, pl.ds(start, 3 * w), :]
        vb_all = vx_ref[0, pl.ds(start, 3 * w), :]
    kc_all, vc_all = kc_ref[0], vc_ref[0]
    for j in range(WIN_KV_HEADS):
        q2 = jnp.concatenate([q[:, (g * j + t) * d:(g * j + t + 1) * d] for t in range(g)], axis=0)
        sk = jnp.zeros((rows, 1), F32)
        for t in range(g):
            sk = jnp.where(row_id // w == t, sink_ref[g * j + t], sk)
        s_ctx = _dot_nt(q2, kc_all[:, j * d:(j + 1) * d]) * scale
        m = jnp.maximum(jnp.max(s_ctx, axis=-1, keepdims=True), sk)
        if band:
            s_loc = _dot_nt(q2, kb_all[:, j * d:(j + 1) * d]) * scale
            s_loc = jnp.where(keep, s_loc, NEG_INF)
            m = jnp.maximum(m, jnp.max(s_loc, axis=-1, keepdims=True))
        p_ctx = jnp.exp(s_ctx - m)
        l = jnp.sum(p_ctx, axis=-1, keepdims=True) + jnp.exp(sk - m)
        o = _dot(p_ctx.astype(BF16), vc_all[:, j * d:(j + 1) * d])
        if band:
            p_loc = jnp.exp(s_loc - m)
            l = l + jnp.sum(p_loc, axis=-1, keepdims=True)
            o = o + _dot(p_loc.astype(BF16), vb_all[:, j * d:(j + 1) * d])
        o = (o * (1.0 / l)).astype(BF16)
        for t in range(g):
            o_ref[0, :, (g * j + t) * d:(g * j + t + 1) * d] = o[t * w:(t + 1) * w, :]


def _outproj_kernel(x_ref, yf_ref, yb_ref, rg_ref, mla_ref, win_ref, g2_ref, sh_ref, sc_ref,
                    n2_ref, wo_ref, rw_ref, xo_ref, h2_ref, afft_ref):
    tm = x_ref.shape[1]
    sub = 128 if tm % 128 == 0 else tm
    for r0 in range(0, tm, sub):
        rows = slice(r0, r0 + sub)
        y = yf_ref[0, rows, :] + yb_ref[0, rows, :]
        parts = []
        for h in range(RET_HEADS):
            yh = y[:, h * RET_DK:(h + 1) * RET_DK]
            mu = jnp.mean(yh, axis=-1, keepdims=True)
            var = jnp.mean(jnp.square(yh - mu), axis=-1, keepdims=True)
            parts.append((yh - mu) * lax.rsqrt(var + GN_EPS))
        ret = jnp.concatenate(parts, axis=1) * _silu(rg_ref[0, rows, :])
        proj = (_dot(ret.astype(BF16), wo_ref[0:256, :]) + _dot(mla_ref[0, rows, :], wo_ref[256:768, :])
                + _dot(win_ref[0, rows, :], wo_ref[768:1024, :]))
        x = x_ref[0, rows, :] + g2_ref[0] * proj
        xo_ref[0, rows, :] = x
        h2 = _rms(x, n2_ref[...]) * (1.0 + sc_ref[0]) + sh_ref[0]
        h2_ref[0, rows, :] = h2
        hi = h2.astype(BF16)
        lo = (h2 - hi.astype(F32)).astype(BF16)
        pp = _dot(jnp.concatenate([hi, lo], axis=0), rw_ref[...])
        logits = (pp[:sub, :LANES] + pp[:sub, LANES:]) + (pp[sub:, :LANES] + pp[sub:, LANES:])
        lt = logits.T[0:N_EXPERTS, :]
        et = jnp.exp(lt - jnp.max(lt, axis=0, keepdims=True))
        afft_ref[0, :, rows] = et / jnp.sum(et, axis=0, keepdims=True)


def _outproj(x, yf, yb, ret, mla, win, mod, mrow, layer, n2, w_out, rw2, tm):
    b, l, d = x.shape

    def mspec(k):
        return pl.BlockSpec((1, 1, d), lambda bi, i: ((layer * MOD_ROWS + mrow(bi)) * 6 + k, 0, 0))

    tok = lambda w: pl.BlockSpec((1, tm, w), lambda bi, i: (bi, i, 0))
    const = lambda a: pl.BlockSpec(a.shape, lambda bi, i: (0,) * a.ndim)
    return pl.pallas_call(
        _outproj_kernel,
        out_shape=[jax.ShapeDtypeStruct((b, l, d), F32), jax.ShapeDtypeStruct((b, l, d), F32),
                   jax.ShapeDtypeStruct((b, N_EXPERTS, l), F32)],
        grid=(b, l // tm),
        in_specs=[tok(d), tok(256), tok(256), pl.BlockSpec((1, tm, 256), lambda bi, i: (bi, i, 3)),
                  tok(512), tok(256), mspec(2), mspec(3), mspec(4), const(n2), const(w_out),
                  const(rw2)],
        out_specs=[tok(d), tok(d), pl.BlockSpec((1, N_EXPERTS, tm), lambda bi, i: (bi, 0, i))],
        compiler_params=_cparams(("arbitrary", "arbitrary")),
        name="outproj",
    )(x, yf, yb, ret, mla, win, mod, mod, mod, n2, w_out, rw2)


def _route_kernel(a_ref, tri_ref, bd_ref, idx_ref, place_ref, exc_ref, inc_ref, *, cap):
    a = a_ref[...]
    ne, nb, _ = a.shape
    grp = bd_ref.shape[0]
    as_f32 = lambda bits: lax.bitcast_convert_type(bits, F32)
    count = lambda m: jnp.sum(jnp.sum(m, axis=1, keepdims=True), axis=2, keepdims=True)

    def earlier_blocks(t):
        tb = t.astype(BF16)
        return jnp.concatenate([_dot(bd_ref[...], tb[r:r + grp, :]) for r in range(0, ne * nb, grp)], axis=0)

    def search(_, c):
        lo, hi = c
        mid = lo + ((hi - lo) >> 1)
        ok = count(jnp.where(a >= as_f32(mid), 1.0, 0.0)) >= cap
        return jnp.where(ok, mid, lo), jnp.where(ok, hi, mid)

    lo, hi = lax.fori_loop(0, 31, search, (jnp.zeros((ne, 1, 1), jnp.int32),
                                           jnp.full((ne, 1, 1), 0x7F800000, jnp.int32)))
    gt3 = jnp.where(a >= as_f32(hi), 1.0, 0.0)
    eq = (jnp.where(a >= as_f32(lo), 1.0, 0.0) - gt3).reshape(ne * nb, LANES)
    gt = gt3.reshape(ne * nb, LANES)
    need = jnp.broadcast_to(cap - count(gt3), a.shape).reshape(ne * nb, LANES)

    cum_eq = _dot(eq.astype(BF16), tri_ref[...])
    tot_eq = jnp.broadcast_to(cum_eq[:, LANES - 1:LANES], cum_eq.shape)
    before = earlier_blocks(tot_eq) + cum_eq - eq
    sel = gt + eq * jnp.where(before < need, 1.0, 0.0)
    selb = sel.astype(BF16)
    cum = _dot(selb, tri_ref[...])
    tot = jnp.broadcast_to(cum[:, LANES - 1:LANES], cum.shape)
    place = sel * (earlier_blocks(tot) + cum)
    high = jnp.floor(place * (1.0 / LANES))
    place_ref[:, 0:LANES] = high
    place_ref[:, LANES:2 * LANES] = place - high * LANES
    earlier = earlier_blocks(sel)
    exc_ref[...] = earlier
    inc_ref[...] = earlier + sel

    slot = lax.broadcasted_iota(jnp.int32, (cap, nb), 0).astype(F32)
    slot1 = (lax.broadcasted_iota(jnp.int32, (cap, LANES), 0) + 1).astype(F32)
    ones8 = jnp.ones((8, LANES), BF16)
    ids = jnp.concatenate([lax.broadcasted_iota(jnp.int32, (8, LANES), 1),
                           lax.broadcasted_iota(jnp.int32, (8, nb), 1) * LANES], axis=1).astype(BF16)

    def per_expert(e, _):
        rows = pl.ds(pl.multiple_of(e * nb, nb), nb)
        inc = _dot_nt(ones8, inc_ref[rows, :].astype(BF16))[0:1, :]
        exc = _dot_nt(ones8, exc_ref[rows, :].astype(BF16))[0:1, :]
        blk = jnp.where(slot >= exc, 1.0, 0.0) * jnp.where(slot < inc, 1.0, 0.0)
        got = _dot(blk.astype(BF16), place_ref[rows, :].astype(BF16))
        match = jnp.where(got[:, 0:LANES] * LANES + got[:, LANES:] == slot1, 1.0, 0.0)
        tok = _dot_nt(ids, jnp.concatenate([match, blk], axis=1).astype(BF16))
        idx_ref[pl.ds(e, 1), :, :] = tok[0:1, :].astype(jnp.int32).reshape(1, 1, cap)
        return 0

    lax.fori_loop(0, ne, per_expert, 0, unroll=4)


def _route(afft, cap):
    b, ne, l = afft.shape
    nb = max(l // LANES, ROUTE_BLOCKS)
    g = b * ne
    a3 = jnp.pad(afft, ((0, 0), (0, 0), (0, nb * LANES - l))).reshape(g, nb, LANES)
    u = np.arange(LANES)
    tri = jnp.asarray(u[:, None] <= u[None, :], BF16)
    r = np.arange(ne * nb)
    bd = jnp.asarray((r[:, None] // nb == r[None, :] // nb) & (r[None, :] % nb < r[:, None] % nb), BF16)
    const = lambda x: pl.BlockSpec(x.shape, lambda i: (0,) * x.ndim)
    idx = pl.pallas_call(
        functools.partial(_route_kernel, cap=cap),
        out_shape=jax.ShapeDtypeStruct((g, 1, cap), jnp.int32),
        grid=(1,),
        in_specs=[const(a3), const(tri), const(bd)],
        out_specs=pl.BlockSpec((g, 1, cap), lambda i: (0, 0, 0)),
        scratch_shapes=[pltpu.VMEM((g * nb, 2 * LANES), F32), pltpu.VMEM((g * nb, LANES), F32),
                        pltpu.VMEM((g * nb, LANES), F32)],
        compiler_params=_cparams(("arbitrary",)),
        name="route",
    )(a3, tri, bd)
    return idx.reshape(b, ne, 1, cap)


def _ffn_kernel(idx_ref, h_ref, wg_ref, wu_ref, wd_ref, ys_ref, xa_ref, xb_ref, *, nb, cap, ne):
    e = pl.program_id(1)

    def gather_rolled(ex, dst):
        for s in range(nb):
            def body(j, _):
                dst[pl.ds(s * cap + j, 1), :] = h_ref[s, pl.ds(idx_ref[s, ex, 0, j], 1), :]
                return 0
            lax.fori_loop(0, cap, body, 0, unroll=8)

    def gather_unrolled(ex, dst):
        for s in range(nb):
            for j in range(cap):
                dst[s * cap + j:s * cap + j + 1, :] = h_ref[s, pl.ds(idx_ref[s, ex, 0, j], 1), :]

    def ffn(src):
        xb = src[...].astype(BF16)
        hid = _silu(_dot(xb, wg_ref[0])) * _dot(xb, wu_ref[0])
        y = _dot(hid.astype(BF16), wd_ref[0])
        for s in range(nb):
            ys_ref[s, 0] = y[s * cap:(s + 1) * cap, :]

    @pl.when(e == 0)
    def _():
        gather_rolled(0, xa_ref)

    e_next = jnp.minimum(e + 1, ne - 1)

    @pl.when(e % 2 == 0)
    def _():
        gather_unrolled(e_next, xb_ref)
        ffn(xa_ref)

    @pl.when(e % 2 == 1)
    def _():
        gather_unrolled(e_next, xa_ref)
        ffn(xb_ref)


def _expert_ffn(idx, h2, wg, wu, wd, layer, nb):
    b, l, d = h2.shape
    ne, cap = idx.shape[1], idx.shape[3]
    ff = wg.shape[2]
    assert ne % 2 == 0
    wsel = lambda bi, e: (layer * ne + e, 0, 0)
    return pl.pallas_call(
        functools.partial(_ffn_kernel, nb=nb, cap=cap, ne=ne),
        out_shape=jax.ShapeDtypeStruct((b, ne, cap, d), F32),
        grid=(b // nb, ne),
        in_specs=[pl.BlockSpec((nb, ne, 1, cap), lambda bi, e: (bi, 0, 0, 0), memory_space=pltpu.SMEM),
                  pl.BlockSpec((nb, l, d), lambda bi, e: (bi, 0, 0), pipeline_mode=pl.Buffered(1)),
                  pl.BlockSpec((1, d, ff), wsel), pl.BlockSpec((1, d, ff), wsel), pl.BlockSpec((1, ff, d), wsel)],
        out_specs=pl.BlockSpec((nb, 1, cap, d), lambda bi, e: (bi, e, 0, 0)),
        scratch_shapes=[pltpu.VMEM((nb * cap, d), F32), pltpu.VMEM((nb * cap, d), F32)],
        compiler_params=_cparams(("arbitrary", "arbitrary")),
        name="expert_ffn",
    )(idx, h2, wg, wu, wd)


def _scatter_kernel(idx_ref, aff_ref, ys_ref, acc_ref, *, nb, cap):
    @pl.when(pl.program_id(1) == 0)
    def _():
        acc_ref[...] = jnp.zeros_like(acc_ref)

    grp = 8
    for s in range(nb):
        def add(gi, _):
            j0 = gi * grp
            ts = [idx_ref[s, 0, 0, j0 + k] for k in range(grp)]
            new = [acc_ref[s, pl.ds(ts[k], 1), :]
                   + ys_ref[s, 0, pl.ds(j0 + k, 1), :] * aff_ref[s, 0, 0, ts[k]] for k in range(grp)]
            for k in range(grp):
                acc_ref[s, pl.ds(ts[k], 1), :] = new[k]
            return 0
        lax.fori_loop(0, cap // grp, add, 0)


def _scatter(idx, afft, ys, nb):
    b, ne, cap, d = ys.shape
    l = afft.shape[2]
    smem = lambda w: pl.BlockSpec((nb, 1, 1, w), lambda bi, e: (bi, e, 0, 0), memory_space=pltpu.SMEM)
    return pl.pallas_call(
        functools.partial(_scatter_kernel, nb=nb, cap=cap),
        out_shape=jax.ShapeDtypeStruct((b, l, d), F32),
        grid=(b // nb, ne),
        in_specs=[smem(cap), smem(l), pl.BlockSpec((nb, 1, cap, d), lambda bi, e: (bi, e, 0, 0))],
        out_specs=pl.BlockSpec((nb, l, d), lambda bi, e: (bi, 0, 0)),
        compiler_params=_cparams(("arbitrary", "arbitrary")),
        name="moe_scatter",
    )(idx, afft.reshape(b, ne, 1, l), ys)


def _final_kernel(x_ref, acc_ref, g5_ref, fg_ref, o_ref):
    o_ref[0] = _rms(x_ref[0] + g5_ref[0] * acc_ref[0], fg_ref[...])


def _final(x, acc, mod, layer, fg, tm):
    b, l, d = x.shape
    tok = pl.BlockSpec((1, tm, d), lambda bi, i: (bi, i, 0))
    return pl.pallas_call(
        _final_kernel,
        out_shape=jax.ShapeDtypeStruct((b, l, d), F32),
        grid=(b, l // tm),
        in_specs=[tok, tok, pl.BlockSpec((1, 1, d), lambda bi, i: ((layer * MOD_ROWS + bi) * 6 + 5, 0, 0)),
                  pl.BlockSpec(fg.shape, lambda bi, i: (0, 0))],
        out_specs=tok,
        compiler_params=_cparams(("arbitrary", "arbitrary")),
        name="final_norm",
    )(x, acc, mod, fg)


def _partner(dh):
    q = dh // 4
    return np.concatenate([np.arange(q, 2 * q), np.arange(0, q), np.arange(3 * q, 4 * q), np.arange(2 * q, 3 * q)])


def _rope_tables(n_tok, dh, reps):
    rows = n_tok // GRID_W
    row = jnp.repeat(jnp.arange(rows, dtype=jnp.int32), GRID_W)
    col = jnp.tile(jnp.arange(GRID_W, dtype=jnp.int32), rows)
    half = dh // 2
    inv = ROPE_BASE ** (-jnp.arange(0, half, 2, dtype=jnp.float32) / half)
    ar = row.astype(jnp.float32)[:, None] * inv[None, :]
    ac = col.astype(jnp.float32)[:, None] * inv[None, :]
    cos = jnp.concatenate([jnp.cos(ar), jnp.cos(ar), jnp.cos(ac), jnp.cos(ac)], axis=1)
    sin = jnp.concatenate([-jnp.sin(ar), jnp.sin(ar), -jnp.sin(ac), jnp.sin(ac)], axis=1)
    return jnp.tile(cos, (1, reps)), jnp.tile(sin, (1, reps))


def _extend_w_in(w):
    d = w.shape[0]
    wq, wk, wv = w[:, 1440:1696], w[:, 1696:1824], w[:, 1824:1952]
    kr = w[:, 1408:1440]
    pq = np.concatenate([h * WIN_DH + _partner(WIN_DH) for h in range(WIN_Q_HEADS)])
    pk = np.concatenate([h * WIN_DH + _partner(WIN_DH) for h in range(WIN_KV_HEADS)])
    ext = jnp.concatenate([w[:, 0:1408], wq, wq[:, pq], wk, wk[:, pk], wv, kr, kr[:, _partner(MLA_ROPE)],
                           jnp.zeros((d, N_EXT - C_KR - 2 * MLA_ROPE), w.dtype)], axis=1)
    return ext.astype(BF16)


def kernel(x, c, ctx, c_ctx, norm1_g, norm2_g, ada_w, ada_b, w_in, ret_decay_f, ret_decay_b,
           mla_qnorm_g, mla_kvnorm_g, mla_w_uq, mla_w_uk, mla_w_uv, win_sink, w_out,
           router_w, exp_w_gate, exp_w_up, exp_w_down, final_g):
    b, l, d = x.shape
    lc = ctx.shape[1]
    depth = w_in.shape[0]
    assert b + 1 <= MOD_ROWS and l % 512 == 0 and lc % RET_CHUNK == 0 and l >= 3 * WINDOW

    c_all = jnp.concatenate([c, c_ctx[None, :], jnp.zeros((MOD_ROWS - b - 1, d), F32)], axis=0)
    mod = _modulation(c_all, ada_w, ada_b).reshape(depth * MOD_ROWS * 6, 1, d)
    row_x = lambda bi: bi
    row_c = lambda bi: b

    wabs = _absorbed_q_weights(mla_w_uq, mla_w_uk)
    cw, sw = _rope_tables(l, WIN_DH, LANES // WIN_DH)
    cm, sm = _rope_tables(l, MLA_ROPE, 1)
    cq, sq = _rope_tables(l, MLA_ROPE, LANES // MLA_ROPE)
    pr = np.concatenate([h * MLA_ROPE + _partner(MLA_ROPE) for h in range(MLA_HEADS)])

    tm = 1024 if l % 1024 == 0 else 512
    acc_x = acc_c = None
    for layer in range(depth):
        need_ctx = layer < depth - 1
        w_ext = _extend_w_in(w_in[layer])
        g1 = norm1_g[layer].reshape(1, d)
        qng = mla_qnorm_g[layer].reshape(1, -1)
        kvg = mla_kvnorm_g[layer].reshape(1, -1)
        uq_rope = mla_w_uq[layer][:, :, MLA_NOPE:].reshape(MLA_Q_RANK, MLA_HEADS * MLA_ROPE)
        wq_all = jnp.concatenate([wabs[layer], uq_rope, uq_rope[:, pr]], axis=1).astype(BF16)
        wuv = jnp.transpose(mla_w_uv[layer], (1, 0, 2)).astype(BF16)
        dec = jnp.broadcast_to(jnp.concatenate([ret_decay_f[layer], ret_decay_b[layer]])[:, None],
                               (2 * RET_HEADS, LANES)).astype(F32)
        wo = w_out[layer].astype(BF16)
        n2 = norm2_g[layer].reshape(1, d)
        rw = jnp.pad(router_w[layer], ((0, 0), (0, LANES - N_EXPERTS)))
        rwh = rw.astype(BF16)
        rw2 = jnp.concatenate([rwh, (rw - rwh.astype(F32)).astype(BF16)], axis=1)
        wg, wu, wd = (w.astype(BF16).reshape((-1,) + w.shape[2:]) for w in (exp_w_gate, exp_w_up, exp_w_down))

        res_x = None if layer == 0 else (acc_x, layer - 1)
        res_c = None if layer == 0 else (acc_c, layer - 1)
        x, ret_x, cqn_x, kcat_x, win_x = _inproj(x, mod, row_x, layer, g1, w_ext, qng, kvg,
                                                 (cw, sw, cm, sm), res_x, tm)
        ctx, ret_c, cqn_c, kcat_c, win_c = _inproj(ctx, mod, row_c, layer, g1, w_ext, qng, kvg,
                                                   None, res_c, min(tm, lc))
        ycf, ycb, yxf, yxb = _retention(ret_c, ret_x, dec, 4 if l % (4 * RET_CHUNK) == 0 else 1)
        mla_x, wn_x = _attention(cqn_x, wq_all, (cq, sq), jnp.concatenate([kcat_c, kcat_x], axis=1), wuv,
                                 win_x, win_x, win_c, win_sink[layer])
        x, h2x, afft_x = _outproj(x, yxf, yxb, ret_x, mla_x, wn_x, mod, row_x, layer, n2, wo, rw2, tm)
        idx_x = _route(afft_x, CAPACITY_FACTOR * l // N_EXPERTS)
        acc_x = _scatter(idx_x, afft_x, _expert_ffn(idx_x, h2x, wg, wu, wd, layer, 1), 1)
        if need_ctx:
            mla_c, wn_c = _attention(cqn_c, wq_all, None, kcat_c, wuv, win_c, None, win_c, win_sink[layer])
            ctx, h2c, afft_c = _outproj(ctx, ycf, ycb, ret_c, mla_c, wn_c, mod, row_c, layer, n2, wo,
                                        rw2, min(tm, lc))
            idx_c = _route(afft_c, CAPACITY_FACTOR * lc // N_EXPERTS)
            acc_c = _scatter(idx_c, afft_c, _expert_ffn(idx_c, h2c, wg, wu, wd, layer, b), b)
    return _final(x, acc_x, mod, depth - 1, final_g.reshape(1, d), tm)
```

```python
import functools

import numpy as np
import jax
import jax.numpy as jnp
from jax import lax
from jax.experimental import pallas as pl
from jax.experimental.pallas import tpu as pltpu

F32 = jnp.float32
BF16 = jnp.bfloat16

D_MODEL = 1024
DEPTH = 2
GRID_W = 64
RET_HEADS = 4
RET_DK = 64
RET_CHUNK = 128
MLA_HEADS = 8
MLA_Q_RANK = 256
MLA_KV_RANK = 128
MLA_NOPE = 64
MLA_ROPE = 32
MLA_DV = 64
WIN_Q_HEADS = 4
WIN_KV_HEADS = 2
WIN_DH = 64
WINDOW = 128
N_EXPERTS = 16
EXPERT_FF = 768
CAPACITY_FACTOR = 2
ROPE_BASE = 10000.0
NORM_EPS = 1e-6
GN_EPS = 1e-5
NEG_INF = -1e30
LOG2E = 1.4426950408889634

LANES = 128
MOD_ROWS = 16
ROUTE_BLOCKS = 32
KCAT = MLA_KV_RANK + MLA_ROPE
VMEM_LIMIT = 56 * 1024 * 1024

C_RET = 0
C_CQ = 1024
C_CKV = 1280
C_WQ = 1408
C_WQP = 1664
C_WK = 1920
C_WKP = 2048
C_WV = 2176
C_KR = 2304
N_EXT = 2432


def _cparams(sem):
    return pltpu.CompilerParams(dimension_semantics=sem, vmem_limit_bytes=VMEM_LIMIT)


def _dot(a, b):
    return jnp.dot(a, b, preferred_element_type=F32)


def _dot_nt(a, b):
    return lax.dot_general(a, b, (((1,), (1,)), ((), ())), preferred_element_type=F32)


def _dot_tn(a, b):
    return lax.dot_general(a, b, (((0,), (0,)), ((), ())), preferred_element_type=F32)


def _rms(x, g):
    return x * lax.rsqrt(jnp.mean(x * x, axis=-1, keepdims=True) + NORM_EPS) * g


def _silu(x):
    return x * jax.nn.sigmoid(x)


def _mod_kernel(c_ref, w_ref, b_ref, o_ref):
    o_ref[0] = jnp.dot(_silu(c_ref[...]), w_ref[0], preferred_element_type=F32,
                       precision=lax.Precision.HIGHEST) + b_ref[0]


def _modulation(c_all, ada_w, ada_b):
    depth, d, n = ada_w.shape
    tn = 1024
    return pl.pallas_call(
        _mod_kernel,
        out_shape=jax.ShapeDtypeStruct((depth, MOD_ROWS, n), F32),
        grid=(depth, n // tn),
        in_specs=[pl.BlockSpec((MOD_ROWS, d), lambda l, j: (0, 0)),
                  pl.BlockSpec((1, d, tn), lambda l, j: (l, 0, j)),
                  pl.BlockSpec((1, 1, tn), lambda l, j: (l, 0, j))],
        out_specs=pl.BlockSpec((1, MOD_ROWS, tn), lambda l, j: (l, 0, j)),
        compiler_params=_cparams(("arbitrary", "arbitrary")),
        name="adaln_mod",
    )(c_all, ada_w, ada_b.reshape(depth, 1, n))


def _wabs_kernel(uq_ref, uk_ref, o_ref):
    o_ref[0] = jnp.dot(uq_ref[0, 0], uk_ref[0, 0], preferred_element_type=F32,
                       precision=lax.Precision.HIGHEST)


def _absorbed_q_weights(w_uq, w_uk):
    depth = w_uq.shape[0]
    uq = jnp.transpose(w_uq[..., :MLA_NOPE], (0, 2, 1, 3))
    uk = jnp.transpose(w_uk, (0, 2, 3, 1))
    return pl.pallas_call(
        _wabs_kernel,
        out_shape=jax.ShapeDtypeStruct((depth, MLA_Q_RANK, MLA_HEADS * MLA_KV_RANK), F32),
        grid=(depth, MLA_HEADS),
        in_specs=[pl.BlockSpec((1, 1, MLA_Q_RANK, MLA_NOPE), lambda l, h: (l, h, 0, 0)),
                  pl.BlockSpec((1, 1, MLA_NOPE, MLA_KV_RANK), lambda l, h: (l, h, 0, 0))],
        out_specs=pl.BlockSpec((1, MLA_Q_RANK, MLA_KV_RANK), lambda l, h: (l, 0, h)),
        compiler_params=_cparams(("arbitrary", "arbitrary")),
        name="mla_absorb",
    )(uq, uk)


def _inproj_kernel(*refs, rope, resid):
    it = iter(refs)
    x_ref = next(it)
    if resid:
        acc_ref, g5_ref = next(it), next(it)
    g1_ref, sh_ref, sc_ref, w_ref, qng_ref, kvg_ref = (next(it) for _ in range(6))
    if rope:
        cw_ref, sw_ref, cm_ref, sm_ref = (next(it) for _ in range(4))
    if resid:
        xo_ref = next(it)
    ret_ref, cqn_ref, kcat_ref, win_ref = (next(it) for _ in range(4))

    x = x_ref[0]
    if resid:
        x = x + g5_ref[0] * acc_ref[0]
        xo_ref[0] = x
    h = _rms(x, g1_ref[...]) * (1.0 + sc_ref[0]) + sh_ref[0]
    p = _dot(h.astype(BF16), w_ref[...])

    ret_ref[0] = p[:, C_RET:C_CQ]
    cqn_ref[0] = _rms(p[:, C_CQ:C_CKV], qng_ref[...]).astype(BF16)
    kvn = _rms(p[:, C_CKV:C_WQ], kvg_ref[...])
    wq, wk, wv = p[:, C_WQ:C_WQP], p[:, C_WK:C_WKP], p[:, C_WV:C_KR]
    kr = p[:, C_KR:C_KR + MLA_ROPE]
    if rope:
        wqp, wkp = p[:, C_WQP:C_WK], p[:, C_WKP:C_WV]
        krp = p[:, C_KR + MLA_ROPE:C_KR + 2 * MLA_ROPE]
        cw, sw = cw_ref[...], sw_ref[...]
        wq = jnp.concatenate([wq[:, :LANES] * cw + wqp[:, :LANES] * sw,
                              wq[:, LANES:] * cw + wqp[:, LANES:] * sw], axis=1)
        wk = wk * cw + wkp * sw
        kr = kr * cm_ref[...] + krp * sm_ref[...]
    kcat_ref[0, :, 0:MLA_KV_RANK] = kvn.astype(BF16)
    kcat_ref[0, :, MLA_KV_RANK:KCAT] = kr.astype(BF16)
    win_ref[0, :, 0:256] = wq.astype(BF16)
    win_ref[0, :, 256:384] = wk.astype(BF16)
    win_ref[0, :, 384:512] = wv.astype(BF16)


def _inproj(x, mod, mrow, layer, g1, w_ext, qng, kvg, tables, resid, tm):
    b, l, d = x.shape
    rope = tables is not None

    def mspec(k, lay):
        return pl.BlockSpec((1, 1, d), lambda bi, i: ((lay * MOD_ROWS + mrow(bi)) * 6 + k, 0, 0))

    tok = lambda w: pl.BlockSpec((1, tm, w), lambda bi, i: (bi, i, 0))
    const = lambda a: pl.BlockSpec(a.shape, lambda bi, i: (0,) * a.ndim)
    args, specs = [x], [tok(d)]
    if resid is not None:
        args += [resid[0], mod]
        specs += [tok(d), mspec(5, resid[1])]
    args += [g1, mod, mod, w_ext, qng, kvg]
    specs += [const(g1), mspec(0, layer), mspec(1, layer), const(w_ext), const(qng), const(kvg)]
    if rope:
        for t in tables:
            args.append(t)
            specs.append(pl.BlockSpec((tm, t.shape[1]), lambda bi, i: (i, 0)))
    out_shape, out_specs = [], []
    if resid is not None:
        out_shape.append(jax.ShapeDtypeStruct((b, l, d), F32))
        out_specs.append(tok(d))
    out_shape += [jax.ShapeDtypeStruct((b, l, 1024), F32), jax.ShapeDtypeStruct((b, l, MLA_Q_RANK), BF16),
                  jax.ShapeDtypeStruct((b, l, KCAT), BF16), jax.ShapeDtypeStruct((b, l, 512), BF16)]
    out_specs += [tok(1024), tok(MLA_Q_RANK), tok(KCAT), tok(512)]
    outs = pl.pallas_call(
        functools.partial(_inproj_kernel, rope=rope, resid=resid is not None),
        out_shape=out_shape, grid=(b, l // tm), in_specs=specs, out_specs=out_specs,
        compiler_params=_cparams(("arbitrary", "arbitrary")),
        name="inproj",
    )(*args)
    if resid is None:
        return (x,) + tuple(outs)
    return tuple(outs)


def _ret_kernel(dec_ref, c_ref, xf_ref, xb_ref, ycf_ref, ycb_ref, yxf_ref, yxb_ref,
                s_ref, dm_ref, qw_ref, kw_ref, gs_ref):
    n = pl.program_id(1)
    c = RET_CHUNK
    dk = RET_DK
    nh = RET_HEADS
    w = nh * dk
    lane_head = lax.broadcasted_iota(jnp.int32, (1, w), 1) // dk
    same_head = (lax.broadcasted_iota(jnp.int32, (w, w), 0) // dk) == (lax.broadcasted_iota(jnp.int32, (w, w), 1) // dk)

    @pl.when(n == 0)
    def _():
        s_ref[...] = jnp.zeros_like(s_ref)
        lg = jnp.log1p(-jnp.exp2(dec_ref[...]))
        ii = lax.broadcasted_iota(jnp.int32, (c, c), 0).astype(F32)
        jj = lax.broadcasted_iota(jnp.int32, (c, c), 1).astype(F32)
        ir = lax.broadcasted_iota(jnp.int32, (c, w), 0).astype(F32)
        row_head = lax.broadcasted_iota(jnp.int32, (w, w), 0) // dk
        for d in range(2):
            lrow = jnp.concatenate([lg[d * nh + h:d * nh + h + 1, 0:dk] for h in range(nh)], axis=1)
            qw_ref[d] = jnp.exp(lrow * (ir + 1.0)) if d == 0 else jnp.exp(lrow * (c - ir))
            kw_ref[d] = jnp.exp(lrow * (c - 1.0 - ir)) if d == 0 else jnp.exp(lrow * ir)
            gs = jnp.zeros((w, w), F32)
            for h in range(nh):
                lh = lg[d * nh + h:d * nh + h + 1, :]
                gs = jnp.where(row_head == h, jnp.exp(jnp.concatenate([lh, lh], axis=1) * float(c)), gs)
                keep = (ii >= jj) if d == 0 else (jj > ii)
                dist = jnp.where(keep, jnp.abs(ii - jj), 0.0)
                dm_ref[d * nh + h] = jnp.where(keep, jnp.exp(lh * dist), 0.0)
            gs_ref[d] = gs

    def run(src_ref, dst_ref, d):
        n_sub = src_ref.shape[1] // c
        order = range(n_sub) if d == 0 else range(n_sub - 1, -1, -1)
        intra, kv, qd = {}, {}, {}
        for g in order:
            rows = slice(g * c, (g + 1) * c)
            q = src_ref[0, rows, 0:w]
            k = src_ref[0, rows, w:2 * w] * (RET_DK ** -0.5)
            v = src_ref[0, rows, 2 * w:3 * w]
            kb, vb = k.astype(BF16), v.astype(BF16)
            scs = [(_dot_nt(jnp.where(lane_head == h, q, 0.0).astype(BF16), kb) * dm_ref[d * nh + h]).astype(BF16)
                   for h in range(nh)]
            v_heads = jnp.concatenate([jnp.where(lane_head == h, v, 0.0).astype(BF16) for h in range(nh)], axis=0)
            intra[g] = _dot(jnp.concatenate(scs, axis=1), v_heads)
            kv[g] = jnp.where(same_head, _dot_tn((k * kw_ref[d]).astype(BF16), vb), 0.0)
            qd[g] = (q * qw_ref[d]).astype(BF16)
        s = s_ref[d]
        for g in order:
            dst_ref[0, g * c:(g + 1) * c, :] = intra[g] + _dot(qd[g], s.astype(BF16))
            s = gs_ref[d] * s + kv[g]
        s_ref[d] = s

    @pl.when(n == 0)
    def _():
        run(c_ref, ycf_ref, 0)
        run(c_ref, ycb_ref, 1)

    @pl.when(n > 0)
    def _():
        run(xf_ref, yxf_ref, 0)
        run(xb_ref, yxb_ref, 1)


def _retention(ret_c, ret_x, dec, g_sub):
    b, lc, _ = ret_c.shape
    lx = ret_x.shape[1]
    c = RET_CHUNK
    blk = g_sub * c
    assert lc % c == 0 and lx % blk == 0
    nx = lx // blk
    ci = lambda bi, n: (bi, 0, 0)
    xfi = lambda bi, n: (bi, jnp.maximum(n - 1, 0), 0)
    xbi = lambda bi, n: (bi, nx - 1 - jnp.maximum(n - 1, 0), 0)
    ydim = RET_HEADS * RET_DK
    return pl.pallas_call(
        _ret_kernel,
        out_shape=[jax.ShapeDtypeStruct((b, lc, ydim), F32), jax.ShapeDtypeStruct((b, lc, ydim), F32),
                   jax.ShapeDtypeStruct((b, lx, ydim), F32), jax.ShapeDtypeStruct((b, lx, ydim), F32)],
        grid=(b, 1 + nx),
        in_specs=[pl.BlockSpec(dec.shape, lambda bi, n: (0, 0)),
                  pl.BlockSpec((1, lc, 768), ci),
                  pl.BlockSpec((1, blk, 768), xfi), pl.BlockSpec((1, blk, 768), xbi)],
        out_specs=[pl.BlockSpec((1, lc, ydim), ci), pl.BlockSpec((1, lc, ydim), ci),
                   pl.BlockSpec((1, blk, ydim), xfi), pl.BlockSpec((1, blk, ydim), xbi)],
        scratch_shapes=[pltpu.VMEM((2, ydim, ydim), F32),
                        pltpu.VMEM((2 * RET_HEADS, c, c), F32),
                        pltpu.VMEM((2, c, ydim), F32),
                        pltpu.VMEM((2, c, ydim), F32),
                        pltpu.VMEM((2, ydim, ydim), F32)],
        compiler_params=_cparams(("arbitrary", "arbitrary")),
        name="retention",
    )(dec, ret_c, ret_x, ret_x)


def _mla_kernel(*refs, rope, band, tq):
    it = iter(refs)
    cqn_ref, wq_ref = next(it), next(it)
    if rope:
        cq_ref, sq_ref = next(it), next(it)
    k_ref, wuv_ref, sink_ref, wnq_ref = (next(it) for _ in range(4))
    wkx_ref, wvx_ref = (next(it), next(it)) if band else (None, None)
    wkc_ref, wvc_ref, o_ref, wno_ref, qcat_ref, s_ref, m_ref = (next(it) for _ in range(7))

    nh = MLA_HEADS
    tk = k_ref.shape[1] // 2
    scale = (MLA_NOPE + MLA_ROPE) ** -0.5 * LOG2E
    qa = _dot(cqn_ref[0], wq_ref[...]) * scale
    qr = qa[:, nh * MLA_KV_RANK:nh * MLA_KV_RANK + nh * MLA_ROPE]
    if rope:
        qp = qa[:, nh * MLA_KV_RANK + nh * MLA_ROPE:]
        cq, sq = cq_ref[...], sq_ref[...]
        qr = jnp.concatenate([qr[:, :LANES] * cq + qp[:, :LANES] * sq,
                              qr[:, LANES:] * cq + qp[:, LANES:] * sq], axis=1)
    for h in range(nh):
        qcat_ref[h * tq:(h + 1) * tq, 0:MLA_KV_RANK] = qa[:, h * MLA_KV_RANK:(h + 1) * MLA_KV_RANK].astype(BF16)
        qcat_ref[h * tq:(h + 1) * tq, MLA_KV_RANK:KCAT] = qr[:, h * MLA_ROPE:(h + 1) * MLA_ROPE].astype(BF16)
    once = jnp.minimum(pl.program_id(1) + 1, 1)

    def run_once(fn):
        lax.fori_loop(0, once, lambda i, c: (fn(), c)[1], 0)

    def pass1():
        q = qcat_ref[...]
        mx = None
        for c in range(2):
            s = _dot_nt(q, k_ref[0, c * tk:(c + 1) * tk, :])
            s_ref[c] = s
            for j in range(0, tk, LANES):
                mx = s[:, j:j + LANES] if mx is None else jnp.maximum(mx, s[:, j:j + LANES])
        m_ref[...] = jnp.broadcast_to(jnp.max(mx, axis=-1, keepdims=True), m_ref.shape)
        _window_block(pl.program_id(1), sink_ref, wnq_ref, wkx_ref, wvx_ref, wkc_ref, wvc_ref, wno_ref)

    def pass2():
        m = m_ref[:, 0:1]
        ones_col = jnp.where(lax.broadcasted_iota(jnp.int32, (tk, LANES), 1) == 0, 1.0, 0.0).astype(BF16)

        def part(c):
            v_aug = jnp.concatenate([k_ref[0, c * tk:(c + 1) * tk, 0:MLA_KV_RANK], ones_col], axis=1)
            return _dot(jnp.exp2(s_ref[c] - m).astype(BF16), v_aug)

        acc = part(0) + part(1)
        o = (acc[:, 0:MLA_KV_RANK] * (1.0 / acc[:, MLA_KV_RANK:MLA_KV_RANK + 1])).astype(BF16)
        for h in range(nh):
            o_ref[0, :, h * MLA_DV:(h + 1) * MLA_DV] = _dot(o[h * tq:(h + 1) * tq, :], wuv_ref[h]).astype(BF16)

    run_once(pass1)
    run_once(pass2)


def _attention(cqn, wq_all, tables, keys, wuv, win_q, win_x, win_c, sink):
    b, l, _ = cqn.shape
    lk = keys.shape[1]
    lc = win_c.shape[1]
    tq = WINDOW
    rope = tables is not None
    band = win_x is not None
    kvw = WIN_KV_HEADS * WIN_DH
    assert lk % (2 * LANES) == 0 and l % tq == 0
    args = [cqn, wq_all]
    specs = [pl.BlockSpec((1, tq, MLA_Q_RANK), lambda bi, i: (bi, i, 0)),
             pl.BlockSpec(wq_all.shape, lambda bi, i: (0, 0))]
    if rope:
        for t in tables:
            args.append(t)
            specs.append(pl.BlockSpec((tq, LANES), lambda bi, i: (i, 0)))
    args.append(keys)
    specs.append(pl.BlockSpec((1, lk, KCAT), lambda bi, i: (bi, 0, 0)))
    args += [wuv, sink, win_q]
    specs += [pl.BlockSpec(wuv.shape, lambda bi, i: (0, 0, 0)), pl.BlockSpec(memory_space=pltpu.SMEM),
              pl.BlockSpec((1, tq, 256), lambda bi, i: (bi, i, 0))]
    if band:
        args += [win_x, win_x]
        specs += [pl.BlockSpec((1, l, kvw), lambda bi, i: (bi, 0, 2)),
                  pl.BlockSpec((1, l, kvw), lambda bi, i: (bi, 0, 3))]
    args += [win_c, win_c]
    specs += [pl.BlockSpec((1, lc, kvw), lambda bi, i: (bi, 0, 2)),
              pl.BlockSpec((1, lc, kvw), lambda bi, i: (bi, 0, 3))]
    rows = MLA_HEADS * tq
    scratch = [pltpu.VMEM((rows, KCAT), BF16), pltpu.VMEM((2, rows, lk // 2), F32),
               pltpu.VMEM((rows, LANES), F32)]
    return pl.pallas_call(
        functools.partial(_mla_kernel, rope=rope, band=band, tq=tq),
        out_shape=[jax.ShapeDtypeStruct((b, l, MLA_HEADS * MLA_DV), BF16),
                   jax.ShapeDtypeStruct((b, l, WIN_Q_HEADS * WIN_DH), BF16)],
        grid=(b, l // tq), in_specs=specs,
        out_specs=[pl.BlockSpec((1, tq, MLA_HEADS * MLA_DV), lambda bi, i: (bi, i, 0)),
                   pl.BlockSpec((1, tq, WIN_Q_HEADS * WIN_DH), lambda bi, i: (bi, i, 0))],
        scratch_shapes=scratch,
        compiler_params=_cparams(("arbitrary", "arbitrary")),
        name="attention",
    )(*args)


def _window_block(n, sink_ref, q_ref, kx_ref, vx_ref, kc_ref, vc_ref, o_ref):
    band = kx_ref is not None
    w = WINDOW
    d = WIN_DH
    g = WIN_Q_HEADS // WIN_KV_HEADS
    scale = d ** -0.5
    q = q_ref[0]
    rows = g * w
    row_id = lax.broadcasted_iota(jnp.int32, (rows, 1), 0)
    if band:
        l_x = kx_ref.shape[1]
        start = pl.multiple_of(jnp.clip((n - 1) * w, 0, l_x - 3 * w), w)
        qi = n * w + lax.broadcasted_iota(jnp.int32, (rows, 3 * w), 0) % w
        kj = start + lax.broadcasted_iota(jnp.int32, (rows, 3 * w), 1)
        keep = jnp.abs(kj - qi) <= WINDOW
        kb_all = kx_ref[0, pl.ds(start, 3 * w), :]
        vb_all = vx_ref[0, pl.ds(start, 3 * w), :]
    kc_all, vc_all = kc_ref[0], vc_ref[0]
    for j in range(WIN_KV_HEADS):
        q2 = jnp.concatenate([q[:, (g * j + t) * d:(g * j + t + 1) * d] for t in range(g)], axis=0)
        sk = jnp.zeros((rows, 1), F32)
        for t in range(g):
            sk = jnp.where(row_id // w == t, sink_ref[g * j + t], sk)
        s_ctx = _dot_nt(q2, kc_all[:, j * d:(j + 1) * d]) * scale
        m = jnp.maximum(jnp.max(s_ctx, axis=-1, keepdims=True), sk)
        if band:
            s_loc = _dot_nt(q2, kb_all[:, j * d:(j + 1) * d]) * scale
            s_loc = jnp.where(keep, s_loc, NEG_INF)
            m = jnp.maximum(m, jnp.max(s_loc, axis=-1, keepdims=True))
        p_ctx = jnp.exp(s_ctx - m)
        l = jnp.sum(p_ctx, axis=-1, keepdims=True) + jnp.exp(sk - m)
        o = _dot(p_ctx.astype(BF16), vc_all[:, j * d:(j + 1) * d])
        if band:
            p_loc = jnp.exp(s_loc - m)
            l = l + jnp.sum(p_loc, axis=-1, keepdims=True)
            o = o + _dot(p_loc.astype(BF16), vb_all[:, j * d:(j + 1) * d])
        o = (o * (1.0 / l)).astype(BF16)
        for t in range(g):
            o_ref[0, :, (g * j + t) * d:(g * j + t + 1) * d] = o[t * w:(t + 1) * w, :]


def _outproj_kernel(x_ref, yf_ref, yb_ref, rg_ref, mla_ref, win_ref, g2_ref, sh_ref, sc_ref,
                    n2_ref, wo_ref, rw_ref, xo_ref, h2_ref, afft_ref):
    tm = x_ref.shape[1]
    sub = 128 if tm % 128 == 0 else tm
    for r0 in range(0, tm, sub):
        rows = slice(r0, r0 + sub)
        y = yf_ref[0, rows, :] + yb_ref[0, rows, :]
        parts = []
        for h in range(RET_HEADS):
            yh = y[:, h * RET_DK:(h + 1) * RET_DK]
            mu = jnp.mean(yh, axis=-1, keepdims=True)
            var = jnp.mean(jnp.square(yh - mu), axis=-1, keepdims=True)
            parts.append((yh - mu) * lax.rsqrt(var + GN_EPS))
        ret = jnp.concatenate(parts, axis=1) * _silu(rg_ref[0, rows, :])
        proj = (_dot(ret.astype(BF16), wo_ref[0:256, :]) + _dot(mla_ref[0, rows, :], wo_ref[256:768, :])
                + _dot(win_ref[0, rows, :], wo_ref[768:1024, :]))
        x = x_ref[0, rows, :] + g2_ref[0] * proj
        xo_ref[0, rows, :] = x
        h2 = _rms(x, n2_ref[...]) * (1.0 + sc_ref[0]) + sh_ref[0]
        h2_ref[0, rows, :] = h2
        hi = h2.astype(BF16)
        lo = (h2 - hi.astype(F32)).astype(BF16)
        pp = _dot(jnp.concatenate([hi, lo], axis=0), rw_ref[...])
        logits = (pp[:sub, :LANES] + pp[:sub, LANES:]) + (pp[sub:, :LANES] + pp[sub:, LANES:])
        lt = logits.T[0:N_EXPERTS, :]
        et = jnp.exp(lt - jnp.max(lt, axis=0, keepdims=True))
        afft_ref[0, :, rows] = et / jnp.sum(et, axis=0, keepdims=True)


def _outproj(x, yf, yb, ret, mla, win, mod, mrow, layer, n2, w_out, rw2, tm):
    b, l, d = x.shape

    def mspec(k):
        return pl.BlockSpec((1, 1, d), lambda bi, i: ((layer * MOD_ROWS + mrow(bi)) * 6 + k, 0, 0))

    tok = lambda w: pl.BlockSpec((1, tm, w), lambda bi, i: (bi, i, 0))
    const = lambda a: pl.BlockSpec(a.shape, lambda bi, i: (0,) * a.ndim)
    return pl.pallas_call(
        _outproj_kernel,
        out_shape=[jax.ShapeDtypeStruct((b, l, d), F32), jax.ShapeDtypeStruct((b, l, d), F32),
                   jax.ShapeDtypeStruct((b, N_EXPERTS, l), F32)],
        grid=(b, l // tm),
        in_specs=[tok(d), tok(256), tok(256), pl.BlockSpec((1, tm, 256), lambda bi, i: (bi, i, 3)),
                  tok(512), tok(256), mspec(2), mspec(3), mspec(4), const(n2), const(w_out),
                  const(rw2)],
        out_specs=[tok(d), tok(d), pl.BlockSpec((1, N_EXPERTS, tm), lambda bi, i: (bi, 0, i))],
        compiler_params=_cparams(("arbitrary", "arbitrary")),
        name="outproj",
    )(x, yf, yb, ret, mla, win, mod, mod, mod, n2, w_out, rw2)


def _route_kernel(a_ref, tri_ref, bd_ref, idx_ref, place_ref, exc_ref, inc_ref, *, cap):
    a = a_ref[...]
    ne, nb, _ = a.shape
    grp = bd_ref.shape[0]
    as_f32 = lambda bits: lax.bitcast_convert_type(bits, F32)
    count = lambda m: jnp.sum(jnp.sum(m, axis=1, keepdims=True), axis=2, keepdims=True)

    def earlier_blocks(t):
        tb = t.astype(BF16)
        return jnp.concatenate([_dot(bd_ref[...], tb[r:r + grp, :]) for r in range(0, ne * nb, grp)], axis=0)

    def search(_, c):
        lo, hi = c
        mid = lo + ((hi - lo) >> 1)
        ok = count(jnp.where(a >= as_f32(mid), 1.0, 0.0)) >= cap
        return jnp.where(ok, mid, lo), jnp.where(ok, hi, mid)

    lo, hi = lax.fori_loop(0, 31, search, (jnp.zeros((ne, 1, 1), jnp.int32),
                                           jnp.full((ne, 1, 1), 0x7F800000, jnp.int32)))
    gt3 = jnp.where(a >= as_f32(hi), 1.0, 0.0)
    eq = (jnp.where(a >= as_f32(lo), 1.0, 0.0) - gt3).reshape(ne * nb, LANES)
    gt = gt3.reshape(ne * nb, LANES)
    need = jnp.broadcast_to(cap - count(gt3), a.shape).reshape(ne * nb, LANES)

    cum_eq = _dot(eq.astype(BF16), tri_ref[...])
    tot_eq = jnp.broadcast_to(cum_eq[:, LANES - 1:LANES], cum_eq.shape)
    before = earlier_blocks(tot_eq) + cum_eq - eq
    sel = gt + eq * jnp.where(before < need, 1.0, 0.0)
    selb = sel.astype(BF16)
    cum = _dot(selb, tri_ref[...])
    tot = jnp.broadcast_to(cum[:, LANES - 1:LANES], cum.shape)
    place = sel * (earlier_blocks(tot) + cum)
    high = jnp.floor(place * (1.0 / LANES))
    place_ref[:, 0:LANES] = high
    place_ref[:, LANES:2 * LANES] = place - high * LANES
    earlier = earlier_blocks(sel)
    exc_ref[...] = earlier
    inc_ref[...] = earlier + sel

    slot = lax.broadcasted_iota(jnp.int32, (cap, nb), 0).astype(F32)
    slot1 = (lax.broadcasted_iota(jnp.int32, (cap, LANES), 0) + 1).astype(F32)
    ones8 = jnp.ones((8, LANES), BF16)
    ids = jnp.concatenate([lax.broadcasted_iota(jnp.int32, (8, LANES), 1),
                           lax.broadcasted_iota(jnp.int32, (8, nb), 1) * LANES], axis=1).astype(BF16)

    def per_expert(e, _):
        rows = pl.ds(pl.multiple_of(e * nb, nb), nb)
        inc = _dot_nt(ones8, inc_ref[rows, :].astype(BF16))[0:1, :]
        exc = _dot_nt(ones8, exc_ref[rows, :].astype(BF16))[0:1, :]
        blk = jnp.where(slot >= exc, 1.0, 0.0) * jnp.where(slot < inc, 1.0, 0.0)
        got = _dot(blk.astype(BF16), place_ref[rows, :].astype(BF16))
        match = jnp.where(got[:, 0:LANES] * LANES + got[:, LANES:] == slot1, 1.0, 0.0)
        tok = _dot_nt(ids, jnp.concatenate([match, blk], axis=1).astype(BF16))
        idx_ref[pl.ds(e, 1), :, :] = tok[0:1, :].astype(jnp.int32).reshape(1, 1, cap)
        return 0

    lax.fori_loop(0, ne, per_expert, 0, unroll=4)


def _route(afft, cap):
    b, ne, l = afft.shape
    nb = max(l // LANES, ROUTE_BLOCKS)
    g = b * ne
    a3 = jnp.pad(afft, ((0, 0), (0, 0), (0, nb * LANES - l))).reshape(g, nb, LANES)
    u = np.arange(LANES)
    tri = jnp.asarray(u[:, None] <= u[None, :], BF16)
    r = np.arange(ne * nb)
    bd = jnp.asarray((r[:, None] // nb == r[None, :] // nb) & (r[None, :] % nb < r[:, None] % nb), BF16)
    const = lambda x: pl.BlockSpec(x.shape, lambda i: (0,) * x.ndim)
    idx = pl.pallas_call(
        functools.partial(_route_kernel, cap=cap),
        out_shape=jax.ShapeDtypeStruct((g, 1, cap), jnp.int32),
        grid=(1,),
        in_specs=[const(a3), const(tri), const(bd)],
        out_specs=pl.BlockSpec((g, 1, cap), lambda i: (0, 0, 0)),
        scratch_shapes=[pltpu.VMEM((g * nb, 2 * LANES), F32), pltpu.VMEM((g * nb, LANES), F32),
                        pltpu.VMEM((g * nb, LANES), F32)],
        compiler_params=_cparams(("arbitrary",)),
        name="route",
    )(a3, tri, bd)
    return idx.reshape(b, ne, 1, cap)


def _moe_kernel(idx_ref, aff_ref, aff_last_ref, h_ref, wg_ref, wu_ref, wd_ref, acc_ref,
                xa_ref, xb_ref, ya_ref, yb_ref, *, nb, cap, ne):
    e = pl.program_id(1)
    grp = 8

    def gather_rolled(ex, dst):
        for s in range(nb):
            def body(j, _):
                dst[pl.ds(s * cap + j, 1), :] = h_ref[s, pl.ds(idx_ref[s, ex, 0, j], 1), :]
                return 0
            lax.fori_loop(0, cap, body, 0, unroll=8)

    def gather_unrolled(ex, dst):
        for s in range(nb):
            for j in range(cap):
                dst[s * cap + j:s * cap + j + 1, :] = h_ref[s, pl.ds(idx_ref[s, ex, 0, j], 1), :]

    def scatter_group(ex, gates, src, s, j0, static):
        ts = [idx_ref[s, ex, 0, j0 + k] for k in range(grp)]
        row = (lambda k: src[s * cap + j0 + k:s * cap + j0 + k + 1, :]) if static else \
              (lambda k: src[pl.ds(s * cap + j0 + k, 1), :])
        new = [acc_ref[s, pl.ds(ts[k], 1), :] + row(k) * gates[s, 0, 0, ts[k]] for k in range(grp)]
        for k in range(grp):
            acc_ref[s, pl.ds(ts[k], 1), :] = new[k]

    def scatter_unrolled(ex, src):
        for s in range(nb):
            for j0 in range(0, cap, grp):
                scatter_group(ex, aff_ref, src, s, j0, True)

    def scatter_rolled(ex, src):
        for s in range(nb):
            def body(gi, _):
                scatter_group(ex, aff_last_ref, src, s, gi * grp, False)
                return 0
            lax.fori_loop(0, cap // grp, body, 0)

    def ffn(src, dst):
        xb = src[...].astype(BF16)
        hid = _silu(_dot(xb, wg_ref[0])) * _dot(xb, wu_ref[0])
        dst[...] = _dot(hid.astype(BF16), wd_ref[0])

    @pl.when(e == 0)
    def _():
        acc_ref[...] = jnp.zeros_like(acc_ref)
        yb_ref[...] = jnp.zeros_like(yb_ref)
        gather_rolled(0, xa_ref)

    e_next = jnp.minimum(e + 1, ne - 1)
    e_prev = jnp.maximum(e - 1, 0)

    @pl.when(e % 2 == 0)
    def _():
        gather_unrolled(e_next, xb_ref)
        ffn(xa_ref, ya_ref)
        scatter_unrolled(e_prev, yb_ref)

    @pl.when(e % 2 == 1)
    def _():
        gather_unrolled(e_next, xa_ref)
        ffn(xb_ref, yb_ref)
        scatter_unrolled(e_prev, ya_ref)

    @pl.when(e == ne - 1)
    def _():
        scatter_rolled(ne - 1, yb_ref)


def _moe(idx, afft, h2, wg, wu, wd, layer, nb):
    b, l, d = h2.shape
    ne, cap = idx.shape[1], idx.shape[3]
    ff = wg.shape[2]
    assert ne % 2 == 0 and cap % 8 == 0
    wsel = lambda bi, e: (layer * ne + e, 0, 0)
    buf = pltpu.VMEM((nb * cap, d), F32)
    aff4 = afft.reshape(b, ne, 1, l)
    return pl.pallas_call(
        functools.partial(_moe_kernel, nb=nb, cap=cap, ne=ne),
        out_shape=jax.ShapeDtypeStruct((b, l, d), F32),
        grid=(b // nb, ne),
        in_specs=[pl.BlockSpec((nb, ne, 1, cap), lambda bi, e: (bi, 0, 0, 0), memory_space=pltpu.SMEM),
                  pl.BlockSpec((nb, 1, 1, l), lambda bi, e: (bi, jnp.maximum(e - 1, 0), 0, 0),
                               memory_space=pltpu.SMEM),
                  pl.BlockSpec((nb, 1, 1, l), lambda bi, e: (bi, ne - 1, 0, 0), memory_space=pltpu.SMEM),
                  pl.BlockSpec((nb, l, d), lambda bi, e: (bi, 0, 0), pipeline_mode=pl.Buffered(1)),
                  pl.BlockSpec((1, d, ff), wsel), pl.BlockSpec((1, d, ff), wsel), pl.BlockSpec((1, ff, d), wsel)],
        out_specs=pl.BlockSpec((nb, l, d), lambda bi, e: (bi, 0, 0), pipeline_mode=pl.Buffered(1)),
        scratch_shapes=[buf, buf, buf, buf],
        compiler_params=_cparams(("arbitrary", "arbitrary")),
        name="expert_mixer",
    )(idx, aff4, aff4, h2, wg, wu, wd)


def _final_kernel(x_ref, acc_ref, g5_ref, fg_ref, o_ref):
    o_ref[0] = _rms(x_ref[0] + g5_ref[0] * acc_ref[0], fg_ref[...])


def _final(x, acc, mod, layer, fg, tm):
    b, l, d = x.shape
    tok = pl.BlockSpec((1, tm, d), lambda bi, i: (bi, i, 0))
    return pl.pallas_call(
        _final_kernel,
        out_shape=jax.ShapeDtypeStruct((b, l, d), F32),
        grid=(b, l // tm),
        in_specs=[tok, tok, pl.BlockSpec((1, 1, d), lambda bi, i: ((layer * MOD_ROWS + bi) * 6 + 5, 0, 0)),
                  pl.BlockSpec(fg.shape, lambda bi, i: (0, 0))],
        out_specs=tok,
        compiler_params=_cparams(("arbitrary", "arbitrary")),
        name="final_norm",
    )(x, acc, mod, fg)


def _partner(dh):
    q = dh // 4
    return np.concatenate([np.arange(q, 2 * q), np.arange(0, q), np.arange(3 * q, 4 * q), np.arange(2 * q, 3 * q)])


def _rope_tables(n_tok, dh, reps):
    rows = n_tok // GRID_W
    row = jnp.repeat(jnp.arange(rows, dtype=jnp.int32), GRID_W)
    col = jnp.tile(jnp.arange(GRID_W, dtype=jnp.int32), rows)
    half = dh // 2
    inv = ROPE_BASE ** (-jnp.arange(0, half, 2, dtype=jnp.float32) / half)
    ar = row.astype(jnp.float32)[:, None] * inv[None, :]
    ac = col.astype(jnp.float32)[:, None] * inv[None, :]
    cos = jnp.concatenate([jnp.cos(ar), jnp.cos(ar), jnp.cos(ac), jnp.cos(ac)], axis=1)
    sin = jnp.concatenate([-jnp.sin(ar), jnp.sin(ar), -jnp.sin(ac), jnp.sin(ac)], axis=1)
    return jnp.tile(cos, (1, reps)), jnp.tile(sin, (1, reps))


def _extend_w_in(w):
    d = w.shape[0]
    wq, wk, wv = w[:, 1440:1696], w[:, 1696:1824], w[:, 1824:1952]
    kr = w[:, 1408:1440]
    pq = np.concatenate([h * WIN_DH + _partner(WIN_DH) for h in range(WIN_Q_HEADS)])
    pk = np.concatenate([h * WIN_DH + _partner(WIN_DH) for h in range(WIN_KV_HEADS)])
    ext = jnp.concatenate([w[:, 0:1408], wq, wq[:, pq], wk, wk[:, pk], wv, kr, kr[:, _partner(MLA_ROPE)],
                           jnp.zeros((d, N_EXT - C_KR - 2 * MLA_ROPE), w.dtype)], axis=1)
    return ext.astype(BF16)


def kernel(x, c, ctx, c_ctx, norm1_g, norm2_g, ada_w, ada_b, w_in, ret_decay_f, ret_decay_b,
           mla_qnorm_g, mla_kvnorm_g, mla_w_uq, mla_w_uk, mla_w_uv, win_sink, w_out,
           router_w, exp_w_gate, exp_w_up, exp_w_down, final_g):
    b, l, d = x.shape
    lc = ctx.shape[1]
    depth = w_in.shape[0]
    assert b + 1 <= MOD_ROWS and l % 512 == 0 and lc % RET_CHUNK == 0 and l >= 3 * WINDOW

    c_all = jnp.concatenate([c, c_ctx[None, :], jnp.zeros((MOD_ROWS - b - 1, d), F32)], axis=0)
    mod = _modulation(c_all, ada_w, ada_b).reshape(depth * MOD_ROWS * 6, 1, d)
    row_x = lambda bi: bi
    row_c = lambda bi: b

    wabs = _absorbed_q_weights(mla_w_uq, mla_w_uk)
    cw, sw = _rope_tables(l, WIN_DH, LANES // WIN_DH)
    cm, sm = _rope_tables(l, MLA_ROPE, 1)
    cq, sq = _rope_tables(l, MLA_ROPE, LANES // MLA_ROPE)
    pr = np.concatenate([h * MLA_ROPE + _partner(MLA_ROPE) for h in range(MLA_HEADS)])

    tm = 1024 if l % 1024 == 0 else 512
    acc_x = acc_c = None
    for layer in range(depth):
        need_ctx = layer < depth - 1
        w_ext = _extend_w_in(w_in[layer])
        g1 = norm1_g[layer].reshape(1, d)
        qng = mla_qnorm_g[layer].reshape(1, -1)
        kvg = mla_kvnorm_g[layer].reshape(1, -1)
        uq_rope = mla_w_uq[layer][:, :, MLA_NOPE:].reshape(MLA_Q_RANK, MLA_HEADS * MLA_ROPE)
        wq_all = jnp.concatenate([wabs[layer], uq_rope, uq_rope[:, pr]], axis=1).astype(BF16)
        wuv = jnp.transpose(mla_w_uv[layer], (1, 0, 2)).astype(BF16)
        dec = jnp.broadcast_to(jnp.concatenate([ret_decay_f[layer], ret_decay_b[layer]])[:, None],
                               (2 * RET_HEADS, LANES)).astype(F32)
        wo = w_out[layer].astype(BF16)
        n2 = norm2_g[layer].reshape(1, d)
        rw = jnp.pad(router_w[layer], ((0, 0), (0, LANES - N_EXPERTS)))
        rwh = rw.astype(BF16)
        rw2 = jnp.concatenate([rwh, (rw - rwh.astype(F32)).astype(BF16)], axis=1)
        wg, wu, wd = (w.astype(BF16).reshape((-1,) + w.shape[2:]) for w in (exp_w_gate, exp_w_up, exp_w_down))

        res_x = None if layer == 0 else (acc_x, layer - 1)
        res_c = None if layer == 0 else (acc_c, layer - 1)
        x, ret_x, cqn_x, kcat_x, win_x = _inproj(x, mod, row_x, layer, g1, w_ext, qng, kvg,
                                                 (cw, sw, cm, sm), res_x, tm)
        ctx, ret_c, cqn_c, kcat_c, win_c = _inproj(ctx, mod, row_c, layer, g1, w_ext, qng, kvg,
                                                   None, res_c, min(tm, lc))
        ycf, ycb, yxf, yxb = _retention(ret_c, ret_x, dec, 4 if l % (4 * RET_CHUNK) == 0 else 1)
        mla_x, wn_x = _attention(cqn_x, wq_all, (cq, sq), jnp.concatenate([kcat_c, kcat_x], axis=1), wuv,
                                 win_x, win_x, win_c, win_sink[layer])
        x, h2x, afft_x = _outproj(x, yxf, yxb, ret_x, mla_x, wn_x, mod, row_x, layer, n2, wo, rw2, tm)
        idx_x = _route(afft_x, CAPACITY_FACTOR * l // N_EXPERTS)
        acc_x = _moe(idx_x, afft_x, h2x, wg, wu, wd, layer, 1)
        if need_ctx:
            mla_c, wn_c = _attention(cqn_c, wq_all, None, kcat_c, wuv, win_c, None, win_c, win_sink[layer])
            ctx, h2c, afft_c = _outproj(ctx, ycf, ycb, ret_c, mla_c, wn_c, mod, row_c, layer, n2, wo,
                                        rw2, min(tm, lc))
            idx_c = _route(afft_c, CAPACITY_FACTOR * lc // N_EXPERTS)
            acc_c = _moe(idx_c, afft_c, h2c, wg, wu, wd, layer, b)
    return _final(x, acc_x, mod, depth - 1, final_g.reshape(1, d), tm)
```

```python
import functools

import numpy as np
import jax
import jax.numpy as jnp
from jax import lax
from jax.experimental import pallas as pl
from jax.experimental.pallas import tpu as pltpu

F32 = jnp.float32
BF16 = jnp.bfloat16

D_MODEL = 1024
DEPTH = 2
GRID_W = 64
RET_HEADS = 4
RET_DK = 64
RET_CHUNK = 128
MLA_HEADS = 8
MLA_Q_RANK = 256
MLA_KV_RANK = 128
MLA_NOPE = 64
MLA_ROPE = 32
MLA_DV = 64
WIN_Q_HEADS = 4
WIN_KV_HEADS = 2
WIN_DH = 64
WINDOW = 128
N_EXPERTS = 16
EXPERT_FF = 768
CAPACITY_FACTOR = 2
ROPE_BASE = 10000.0
NORM_EPS = 1e-6
GN_EPS = 1e-5
NEG_INF = -1e30
LOG2E = 1.4426950408889634

LANES = 128
MOD_ROWS = 16
ROUTE_BLOCKS = 32
KCAT = MLA_KV_RANK + MLA_ROPE
VMEM_LIMIT = 56 * 1024 * 1024

C_RET = 0
C_CQ = 1024
C_CKV = 1280
C_WQ = 1408
C_WQP = 1664
C_WK = 1920
C_WKP = 2048
C_WV = 2176
C_KR = 2304
N_EXT = 2432


def _cparams(sem):
    return pltpu.CompilerParams(dimension_semantics=sem, vmem_limit_bytes=VMEM_LIMIT)


def _dot(a, b):
    return jnp.dot(a, b, preferred_element_type=F32)


def _dot_nt(a, b):
    return lax.dot_general(a, b, (((1,), (1,)), ((), ())), preferred_element_type=F32)


def _dot_tn(a, b):
    return lax.dot_general(a, b, (((0,), (0,)), ((), ())), preferred_element_type=F32)


def _rms(x, g):
    return x * lax.rsqrt(jnp.mean(x * x, axis=-1, keepdims=True) + NORM_EPS) * g


def _silu(x):
    return x * jax.nn.sigmoid(x)


def _mod_kernel(c_ref, w_ref, b_ref, o_ref):
    o_ref[0] = jnp.dot(_silu(c_ref[...]), w_ref[0], preferred_element_type=F32,
                       precision=lax.Precision.HIGHEST) + b_ref[0]


def _modulation(c_all, ada_w, ada_b):
    depth, d, n = ada_w.shape
    tn = 1024
    return pl.pallas_call(
        _mod_kernel,
        out_shape=jax.ShapeDtypeStruct((depth, MOD_ROWS, n), F32),
        grid=(depth, n // tn),
        in_specs=[pl.BlockSpec((MOD_ROWS, d), lambda l, j: (0, 0)),
                  pl.BlockSpec((1, d, tn), lambda l, j: (l, 0, j)),
                  pl.BlockSpec((1, 1, tn), lambda l, j: (l, 0, j))],
        out_specs=pl.BlockSpec((1, MOD_ROWS, tn), lambda l, j: (l, 0, j)),
        compiler_params=_cparams(("arbitrary", "arbitrary")),
        name="adaln_mod",
    )(c_all, ada_w, ada_b.reshape(depth, 1, n))


def _wabs_kernel(uq_ref, uk_ref, o_ref):
    o_ref[0] = jnp.dot(uq_ref[0, 0], uk_ref[0, 0], preferred_element_type=F32,
                       precision=lax.Precision.HIGHEST)


def _absorbed_q_weights(w_uq, w_uk):
    depth = w_uq.shape[0]
    uq = jnp.transpose(w_uq[..., :MLA_NOPE], (0, 2, 1, 3))
    uk = jnp.transpose(w_uk, (0, 2, 3, 1))
    return pl.pallas_call(
        _wabs_kernel,
        out_shape=jax.ShapeDtypeStruct((depth, MLA_Q_RANK, MLA_HEADS * MLA_KV_RANK), F32),
        grid=(depth, MLA_HEADS),
        in_specs=[pl.BlockSpec((1, 1, MLA_Q_RANK, MLA_NOPE), lambda l, h: (l, h, 0, 0)),
                  pl.BlockSpec((1, 1, MLA_NOPE, MLA_KV_RANK), lambda l, h: (l, h, 0, 0))],
        out_specs=pl.BlockSpec((1, MLA_Q_RANK, MLA_KV_RANK), lambda l, h: (l, 0, h)),
        compiler_params=_cparams(("arbitrary", "arbitrary")),
        name="mla_absorb",
    )(uq, uk)


def _inproj_kernel(*refs, rope, resid):
    it = iter(refs)
    x_ref = next(it)
    if resid:
        acc_ref, g5_ref = next(it), next(it)
    g1_ref, sh_ref, sc_ref, w_ref, qng_ref, kvg_ref = (next(it) for _ in range(6))
    if rope:
        cw_ref, sw_ref, cm_ref, sm_ref = (next(it) for _ in range(4))
    if resid:
        xo_ref = next(it)
    ret_ref, cqn_ref, kcat_ref, win_ref = (next(it) for _ in range(4))

    x = x_ref[0]
    if resid:
        x = x + g5_ref[0] * acc_ref[0]
        xo_ref[0] = x
    h = _rms(x, g1_ref[...]) * (1.0 + sc_ref[0]) + sh_ref[0]
    p = _dot(h.astype(BF16), w_ref[...])

    ret_ref[0] = p[:, C_RET:C_CQ]
    cqn_ref[0] = _rms(p[:, C_CQ:C_CKV], qng_ref[...]).astype(BF16)
    kvn = _rms(p[:, C_CKV:C_WQ], kvg_ref[...])
    wq, wk, wv = p[:, C_WQ:C_WQP], p[:, C_WK:C_WKP], p[:, C_WV:C_KR]
    kr = p[:, C_KR:C_KR + MLA_ROPE]
    if rope:
        wqp, wkp = p[:, C_WQP:C_WK], p[:, C_WKP:C_WV]
        krp = p[:, C_KR + MLA_ROPE:C_KR + 2 * MLA_ROPE]
        cw, sw = cw_ref[...], sw_ref[...]
        wq = jnp.concatenate([wq[:, :LANES] * cw + wqp[:, :LANES] * sw,
                              wq[:, LANES:] * cw + wqp[:, LANES:] * sw], axis=1)
        wk = wk * cw + wkp * sw
        kr = kr * cm_ref[...] + krp * sm_ref[...]
    kcat_ref[0, :, 0:MLA_KV_RANK] = kvn.astype(BF16)
    kcat_ref[0, :, MLA_KV_RANK:KCAT] = kr.astype(BF16)
    win_ref[0, :, 0:256] = wq.astype(BF16)
    win_ref[0, :, 256:384] = wk.astype(BF16)
    win_ref[0, :, 384:512] = wv.astype(BF16)


def _inproj(x, mod, mrow, layer, g1, w_ext, qng, kvg, tables, resid, tm):
    b, l, d = x.shape
    rope = tables is not None

    def mspec(k, lay):
        return pl.BlockSpec((1, 1, d), lambda bi, i: ((lay * MOD_ROWS + mrow(bi)) * 6 + k, 0, 0))

    tok = lambda w: pl.BlockSpec((1, tm, w), lambda bi, i: (bi, i, 0))
    const = lambda a: pl.BlockSpec(a.shape, lambda bi, i: (0,) * a.ndim)
    args, specs = [x], [tok(d)]
    if resid is not None:
        args += [resid[0], mod]
        specs += [tok(d), mspec(5, resid[1])]
    args += [g1, mod, mod, w_ext, qng, kvg]
    specs += [const(g1), mspec(0, layer), mspec(1, layer), const(w_ext), const(qng), const(kvg)]
    if rope:
        for t in tables:
            args.append(t)
            specs.append(pl.BlockSpec((tm, t.shape[1]), lambda bi, i: (i, 0)))
    out_shape, out_specs = [], []
    if resid is not None:
        out_shape.append(jax.ShapeDtypeStruct((b, l, d), F32))
        out_specs.append(tok(d))
    out_shape += [jax.ShapeDtypeStruct((b, l, 1024), F32), jax.ShapeDtypeStruct((b, l, MLA_Q_RANK), BF16),
                  jax.ShapeDtypeStruct((b, l, KCAT), BF16), jax.ShapeDtypeStruct((b, l, 512), BF16)]
    out_specs += [tok(1024), tok(MLA_Q_RANK), tok(KCAT), tok(512)]
    outs = pl.pallas_call(
        functools.partial(_inproj_kernel, rope=rope, resid=resid is not None),
        out_shape=out_shape, grid=(b, l // tm), in_specs=specs, out_specs=out_specs,
        compiler_params=_cparams(("arbitrary", "arbitrary")),
        name="inproj",
    )(*args)
    if resid is None:
        return (x,) + tuple(outs)
    return tuple(outs)


def _ret_kernel(dec_ref, c_ref, xf_ref, xb_ref, ycf_ref, ycb_ref, yxf_ref, yxb_ref,
                s_ref, dm_ref, qw_ref, kw_ref, gs_ref):
    n = pl.program_id(1)
    c = RET_CHUNK
    dk = RET_DK
    nh = RET_HEADS
    w = nh * dk
    lane_head = lax.broadcasted_iota(jnp.int32, (1, w), 1) // dk
    same_head = (lax.broadcasted_iota(jnp.int32, (w, w), 0) // dk) == (lax.broadcasted_iota(jnp.int32, (w, w), 1) // dk)

    @pl.when(n == 0)
    def _():
        s_ref[...] = jnp.zeros_like(s_ref)
        lg = jnp.log1p(-jnp.exp2(dec_ref[...]))
        ii = lax.broadcasted_iota(jnp.int32, (c, c), 0).astype(F32)
        jj = lax.broadcasted_iota(jnp.int32, (c, c), 1).astype(F32)
        ir = lax.broadcasted_iota(jnp.int32, (c, w), 0).astype(F32)
        row_head = lax.broadcasted_iota(jnp.int32, (w, w), 0) // dk
        for d in range(2):
            lrow = jnp.concatenate([lg[d * nh + h:d * nh + h + 1, 0:dk] for h in range(nh)], axis=1)
            qw_ref[d] = jnp.exp(lrow * (ir + 1.0)) if d == 0 else jnp.exp(lrow * (c - ir))
            kw_ref[d] = jnp.exp(lrow * (c - 1.0 - ir)) if d == 0 else jnp.exp(lrow * ir)
            gs = jnp.zeros((w, w), F32)
            for h in range(nh):
                lh = lg[d * nh + h:d * nh + h + 1, :]
                gs = jnp.where(row_head == h, jnp.exp(jnp.concatenate([lh, lh], axis=1) * float(c)), gs)
                keep = (ii >= jj) if d == 0 else (jj > ii)
                dist = jnp.where(keep, jnp.abs(ii - jj), 0.0)
                dm_ref[d * nh + h] = jnp.where(keep, jnp.exp(lh * dist), 0.0)
            gs_ref[d] = gs

    def run(src_ref, dst_ref, d):
        n_sub = src_ref.shape[1] // c
        order = range(n_sub) if d == 0 else range(n_sub - 1, -1, -1)
        intra, kv, qd = {}, {}, {}
        for g in order:
            rows = slice(g * c, (g + 1) * c)
            q = src_ref[0, rows, 0:w]
            k = src_ref[0, rows, w:2 * w] * (RET_DK ** -0.5)
            v = src_ref[0, rows, 2 * w:3 * w]
            kb, vb = k.astype(BF16), v.astype(BF16)
            scs = [(_dot_nt(jnp.where(lane_head == h, q, 0.0).astype(BF16), kb) * dm_ref[d * nh + h]).astype(BF16)
                   for h in range(nh)]
            v_heads = jnp.concatenate([jnp.where(lane_head == h, v, 0.0).astype(BF16) for h in range(nh)], axis=0)
            intra[g] = _dot(jnp.concatenate(scs, axis=1), v_heads)
            kv[g] = jnp.where(same_head, _dot_tn((k * kw_ref[d]).astype(BF16), vb), 0.0)
            qd[g] = (q * qw_ref[d]).astype(BF16)
        s = s_ref[d]
        for g in order:
            dst_ref[0, g * c:(g + 1) * c, :] = intra[g] + _dot(qd[g], s.astype(BF16))
            s = gs_ref[d] * s + kv[g]
        s_ref[d] = s

    @pl.when(n == 0)
    def _():
        run(c_ref, ycf_ref, 0)
        run(c_ref, ycb_ref, 1)

    @pl.when(n > 0)
    def _():
        run(xf_ref, yxf_ref, 0)
        run(xb_ref, yxb_ref, 1)


def _retention(ret_c, ret_x, dec, g_sub):
    b, lc, _ = ret_c.shape
    lx = ret_x.shape[1]
    c = RET_CHUNK
    blk = g_sub * c
    assert lc % c == 0 and lx % blk == 0
    nx = lx // blk
    ci = lambda bi, n: (bi, 0, 0)
    xfi = lambda bi, n: (bi, jnp.maximum(n - 1, 0), 0)
    xbi = lambda bi, n: (bi, nx - 1 - jnp.maximum(n - 1, 0), 0)
    ydim = RET_HEADS * RET_DK
    return pl.pallas_call(
        _ret_kernel,
        out_shape=[jax.ShapeDtypeStruct((b, lc, ydim), F32), jax.ShapeDtypeStruct((b, lc, ydim), F32),
                   jax.ShapeDtypeStruct((b, lx, ydim), F32), jax.ShapeDtypeStruct((b, lx, ydim), F32)],
        grid=(b, 1 + nx),
        in_specs=[pl.BlockSpec(dec.shape, lambda bi, n: (0, 0)),
                  pl.BlockSpec((1, lc, 768), ci),
                  pl.BlockSpec((1, blk, 768), xfi), pl.BlockSpec((1, blk, 768), xbi)],
        out_specs=[pl.BlockSpec((1, lc, ydim), ci), pl.BlockSpec((1, lc, ydim), ci),
                   pl.BlockSpec((1, blk, ydim), xfi), pl.BlockSpec((1, blk, ydim), xbi)],
        scratch_shapes=[pltpu.VMEM((2, ydim, ydim), F32),
                        pltpu.VMEM((2 * RET_HEADS, c, c), F32),
                        pltpu.VMEM((2, c, ydim), F32),
                        pltpu.VMEM((2, c, ydim), F32),
                        pltpu.VMEM((2, ydim, ydim), F32)],
        compiler_params=_cparams(("arbitrary", "arbitrary")),
        name="retention",
    )(dec, ret_c, ret_x, ret_x)


def _mla_kernel(*refs, rope, band, tq):
    it = iter(refs)
    cqn_ref, wq_ref = next(it), next(it)
    if rope:
        cq_ref, sq_ref = next(it), next(it)
    k_ref, wuv_ref, sink_ref, wnq_ref = (next(it) for _ in range(4))
    wkx_ref, wvx_ref = (next(it), next(it)) if band else (None, None)
    wkc_ref, wvc_ref, o_ref, wno_ref, qcat_ref, s_ref, m_ref = (next(it) for _ in range(7))

    nh = MLA_HEADS
    tk = k_ref.shape[1] // 2
    scale = (MLA_NOPE + MLA_ROPE) ** -0.5 * LOG2E
    qa = _dot(cqn_ref[0], wq_ref[...]) * scale
    qr = qa[:, nh * MLA_KV_RANK:nh * MLA_KV_RANK + nh * MLA_ROPE]
    if rope:
        qp = qa[:, nh * MLA_KV_RANK + nh * MLA_ROPE:]
        cq, sq = cq_ref[...], sq_ref[...]
        qr = jnp.concatenate([qr[:, :LANES] * cq + qp[:, :LANES] * sq,
                              qr[:, LANES:] * cq + qp[:, LANES:] * sq], axis=1)
    for h in range(nh):
        qcat_ref[h * tq:(h + 1) * tq, 0:MLA_KV_RANK] = qa[:, h * MLA_KV_RANK:(h + 1) * MLA_KV_RANK].astype(BF16)
        qcat_ref[h * tq:(h + 1) * tq, MLA_KV_RANK:KCAT] = qr[:, h * MLA_ROPE:(h + 1) * MLA_ROPE].astype(BF16)
    once = jnp.minimum(pl.program_id(1) + 1, 1)

    def run_once(fn):
        lax.fori_loop(0, once, lambda i, c: (fn(), c)[1], 0)

    def pass1():
        q = qcat_ref[...]
        mx = None
        for c in range(2):
            s = _dot_nt(q, k_ref[0, c * tk:(c + 1) * tk, :])
            s_ref[c] = s
            for j in range(0, tk, LANES):
                mx = s[:, j:j + LANES] if mx is None else jnp.maximum(mx, s[:, j:j + LANES])
        m_ref[...] = jnp.broadcast_to(jnp.max(mx, axis=-1, keepdims=True), m_ref.shape)
        _window_block(pl.program_id(1), sink_ref, wnq_ref, wkx_ref, wvx_ref, wkc_ref, wvc_ref, wno_ref)

    def pass2():
        m = m_ref[:, 0:1]
        ones_col = jnp.where(lax.broadcasted_iota(jnp.int32, (tk, LANES), 1) == 0, 1.0, 0.0).astype(BF16)

        def part(c):
            v_aug = jnp.concatenate([k_ref[0, c * tk:(c + 1) * tk, 0:MLA_KV_RANK], ones_col], axis=1)
            return _dot(jnp.exp2(s_ref[c] - m).astype(BF16), v_aug)

        acc = part(0) + part(1)
        o = (acc[:, 0:MLA_KV_RANK] * (1.0 / acc[:, MLA_KV_RANK:MLA_KV_RANK + 1])).astype(BF16)
        for h in range(nh):
            o_ref[0, :, h * MLA_DV:(h + 1) * MLA_DV] = _dot(o[h * tq:(h + 1) * tq, :], wuv_ref[h]).astype(BF16)

    run_once(pass1)
    run_once(pass2)


def _attention(cqn, wq_all, tables, keys, wuv, win_q, win_x, win_c, sink):
    b, l, _ = cqn.shape
    lk = keys.shape[1]
    lc = win_c.shape[1]
    tq = WINDOW
    rope = tables is not None
    band = win_x is not None
    kvw = WIN_KV_HEADS * WIN_DH
    assert lk % (2 * LANES) == 0 and l % tq == 0
    args = [cqn, wq_all]
    specs = [pl.BlockSpec((1, tq, MLA_Q_RANK), lambda bi, i: (bi, i, 0)),
             pl.BlockSpec(wq_all.shape, lambda bi, i: (0, 0))]
    if rope:
        for t in tables:
            args.append(t)
            specs.append(pl.BlockSpec((tq, LANES), lambda bi, i: (i, 0)))
    args.append(keys)
    specs.append(pl.BlockSpec((1, lk, KCAT), lambda bi, i: (bi, 0, 0)))
    args += [wuv, sink, win_q]
    specs += [pl.BlockSpec(wuv.shape, lambda bi, i: (0, 0, 0)), pl.BlockSpec(memory_space=pltpu.SMEM),
              pl.BlockSpec((1, tq, 256), lambda bi, i: (bi, i, 0))]
    if band:
        args += [win_x, win_x]
        specs += [pl.BlockSpec((1, l, kvw), lambda bi, i: (bi, 0, 2)),
                  pl.BlockSpec((1, l, kvw), lambda bi, i: (bi, 0, 3))]
    args += [win_c, win_c]
    specs += [pl.BlockSpec((1, lc, kvw), lambda bi, i: (bi, 0, 2)),
              pl.BlockSpec((1, lc, kvw), lambda bi, i: (bi, 0, 3))]
    rows = MLA_HEADS * tq
    scratch = [pltpu.VMEM((rows, KCAT), BF16), pltpu.VMEM((2, rows, lk // 2), F32),
               pltpu.VMEM((rows, LANES), F32)]
    return pl.pallas_call(
        functools.partial(_mla_kernel, rope=rope, band=band, tq=tq),
        out_shape=[jax.ShapeDtypeStruct((b, l, MLA_HEADS * MLA_DV), BF16),
                   jax.ShapeDtypeStruct((b, l, WIN_Q_HEADS * WIN_DH), BF16)],
        grid=(b, l // tq), in_specs=specs,
        out_specs=[pl.BlockSpec((1, tq, MLA_HEADS * MLA_DV), lambda bi, i: (bi, i, 0)),
                   pl.BlockSpec((1, tq, WIN_Q_HEADS * WIN_DH), lambda bi, i: (bi, i, 0))],
        scratch_shapes=scratch,
        compiler_params=_cparams(("arbitrary", "arbitrary")),
        name="attention",
    )(*args)


def _window_block(n, sink_ref, q_ref, kx_ref, vx_ref, kc_ref, vc_ref, o_ref):
    band = kx_ref is not None
    w = WINDOW
    d = WIN_DH
    g = WIN_Q_HEADS // WIN_KV_HEADS
    scale = d ** -0.5
    q = q_ref[0]
    rows = g * w
    row_id = lax.broadcasted_iota(jnp.int32, (rows, 1), 0)
    if band:
        l_x = kx_ref.shape[1]
        start = pl.multiple_of(jnp.clip((n - 1) * w, 0, l_x - 3 * w), w)
        qi = n * w + lax.broadcasted_iota(jnp.int32, (rows, 3 * w), 0) % w
        kj = start + lax.broadcasted_iota(jnp.int32, (rows, 3 * w), 1)
        keep = jnp.abs(kj - qi) <= WINDOW
        kb_all = kx_ref[0, pl.ds(start, 3 * w), :]
        vb_all = vx_ref[0, pl.ds(start, 3 * w), :]
    kc_all, vc_all = kc_ref[0], vc_ref[0]
    for j in range(WIN_KV_HEADS):
        q2 = jnp.concatenate([q[:, (g * j + t) * d:(g * j + t + 1) * d] for t in range(g)], axis=0)
        sk = jnp.zeros((rows, 1), F32)
        for t in range(g):
            sk = jnp.where(row_id // w == t, sink_ref[g * j + t], sk)
        s_ctx = _dot_nt(q2, kc_all[:, j * d:(j + 1) * d]) * scale
        m = jnp.maximum(jnp.max(s_ctx, axis=-1, keepdims=True), sk)
        if band:
            s_loc = _dot_nt(q2, kb_all[:, j * d:(j + 1) * d]) * scale
            s_loc = jnp.where(keep, s_loc, NEG_INF)
            m = jnp.maximum(m, jnp.max(s_loc, axis=-1, keepdims=True))
        p_ctx = jnp.exp(s_ctx - m)
        l = jnp.sum(p_ctx, axis=-1, keepdims=True) + jnp.exp(sk - m)
        o = _dot(p_ctx.astype(BF16), vc_all[:, j * d:(j + 1) * d])
        if band:
            p_loc = jnp.exp(s_loc - m)
            l = l + jnp.sum(p_loc, axis=-1, keepdims=True)
            o = o + _dot(p_loc.astype(BF16), vb_all[:, j * d:(j + 1) * d])
        o = (o * (1.0 / l)).astype(BF16)
        for t in range(g):
            o_ref[0, :, (g * j + t) * d:(g * j + t + 1) * d] = o[t * w:(t + 1) * w, :]


def _outproj_kernel(x_ref, yf_ref, yb_ref, rg_ref, mla_ref, win_ref, g2_ref, sh_ref, sc_ref,
                    n2_ref, wo_ref, rw_ref, xo_ref, h2_ref, afft_ref):
    tm = x_ref.shape[1]
    sub = 128 if tm % 128 == 0 else tm
    for r0 in range(0, tm, sub):
        rows = slice(r0, r0 + sub)
        y = yf_ref[0, rows, :] + yb_ref[0, rows, :]
        parts = []
        for h in range(RET_HEADS):
            yh = y[:, h * RET_DK:(h + 1) * RET_DK]
            mu = jnp.mean(yh, axis=-1, keepdims=True)
            var = jnp.mean(jnp.square(yh - mu), axis=-1, keepdims=True)
            parts.append((yh - mu) * lax.rsqrt(var + GN_EPS))
        ret = jnp.concatenate(parts, axis=1) * _silu(rg_ref[0, rows, :])
        proj = (_dot(ret.astype(BF16), wo_ref[0:256, :]) + _dot(mla_ref[0, rows, :], wo_ref[256:768, :])
                + _dot(win_ref[0, rows, :], wo_ref[768:1024, :]))
        x = x_ref[0, rows, :] + g2_ref[0] * proj
        xo_ref[0, rows, :] = x
        h2 = _rms(x, n2_ref[...]) * (1.0 + sc_ref[0]) + sh_ref[0]
        h2_ref[0, rows, :] = h2
        hi = h2.astype(BF16)
        lo = (h2 - hi.astype(F32)).astype(BF16)
        pp = _dot(jnp.concatenate([hi, lo], axis=0), rw_ref[...])
        logits = (pp[:sub, :LANES] + pp[:sub, LANES:]) + (pp[sub:, :LANES] + pp[sub:, LANES:])
        lt = logits.T[0:N_EXPERTS, :]
        et = jnp.exp(lt - jnp.max(lt, axis=0, keepdims=True))
        afft_ref[0, :, rows] = et / jnp.sum(et, axis=0, keepdims=True)


def _outproj(x, yf, yb, ret, mla, win, mod, mrow, layer, n2, w_out, rw2, tm):
    b, l, d = x.shape

    def mspec(k):
        return pl.BlockSpec((1, 1, d), lambda bi, i: ((layer * MOD_ROWS + mrow(bi)) * 6 + k, 0, 0))

    tok = lambda w: pl.BlockSpec((1, tm, w), lambda bi, i: (bi, i, 0))
    const = lambda a: pl.BlockSpec(a.shape, lambda bi, i: (0,) * a.ndim)
    return pl.pallas_call(
        _outproj_kernel,
        out_shape=[jax.ShapeDtypeStruct((b, l, d), F32), jax.ShapeDtypeStruct((b, l, d), F32),
                   jax.ShapeDtypeStruct((b, N_EXPERTS, l), F32)],
        grid=(b, l // tm),
        in_specs=[tok(d), tok(256), tok(256), pl.BlockSpec((1, tm, 256), lambda bi, i: (bi, i, 3)),
                  tok(512), tok(256), mspec(2), mspec(3), mspec(4), const(n2), const(w_out),
                  const(rw2)],
        out_specs=[tok(d), tok(d), pl.BlockSpec((1, N_EXPERTS, tm), lambda bi, i: (bi, 0, i))],
        compiler_params=_cparams(("arbitrary", "arbitrary")),
        name="outproj",
    )(x, yf, yb, ret, mla, win, mod, mod, mod, n2, w_out, rw2)


def _route_kernel(a_ref, tri_ref, bd_ref, idx_ref, place_ref, exc_ref, inc_ref, *, cap):
    a = a_ref[...]
    ne, nb, _ = a.shape
    grp = bd_ref.shape[0]
    as_f32 = lambda bits: lax.bitcast_convert_type(bits, F32)
    count = lambda m: jnp.sum(jnp.sum(m, axis=1, keepdims=True), axis=2, keepdims=True)

    def earlier_blocks(t):
        tb = t.astype(BF16)
        return jnp.concatenate([_dot(bd_ref[...], tb[r:r + grp, :]) for r in range(0, ne * nb, grp)], axis=0)

    def search(_, c):
        lo, hi = c
        mid = lo + ((hi - lo) >> 1)
        ok = count(jnp.where(a >= as_f32(mid), 1.0, 0.0)) >= cap
        return jnp.where(ok, mid, lo), jnp.where(ok, hi, mid)

    lo, hi = lax.fori_loop(0, 31, search, (jnp.zeros((ne, 1, 1), jnp.int32),
                                           jnp.full((ne, 1, 1), 0x7F800000, jnp.int32)))
    gt3 = jnp.where(a >= as_f32(hi), 1.0, 0.0)
    eq = (jnp.where(a >= as_f32(lo), 1.0, 0.0) - gt3).reshape(ne * nb, LANES)
    gt = gt3.reshape(ne * nb, LANES)
    need = jnp.broadcast_to(cap - count(gt3), a.shape).reshape(ne * nb, LANES)

    cum_eq = _dot(eq.astype(BF16), tri_ref[...])
    tot_eq = jnp.broadcast_to(cum_eq[:, LANES - 1:LANES], cum_eq.shape)
    before = earlier_blocks(tot_eq) + cum_eq - eq
    sel = gt + eq * jnp.where(before < need, 1.0, 0.0)
    selb = sel.astype(BF16)
    cum = _dot(selb, tri_ref[...])
    tot = jnp.broadcast_to(cum[:, LANES - 1:LANES], cum.shape)
    place = sel * (earlier_blocks(tot) + cum)
    high = jnp.floor(place * (1.0 / LANES))
    place_ref[:, 0:LANES] = high
    place_ref[:, LANES:2 * LANES] = place - high * LANES
    earlier = earlier_blocks(sel)
    exc_ref[...] = earlier
    inc_ref[...] = earlier + sel

    slot = lax.broadcasted_iota(jnp.int32, (cap, nb), 0).astype(F32)
    slot1 = (lax.broadcasted_iota(jnp.int32, (cap, LANES), 0) + 1).astype(F32)
    ones8 = jnp.ones((8, LANES), BF16)
    ids = jnp.concatenate([lax.broadcasted_iota(jnp.int32, (8, LANES), 1),
                           lax.broadcasted_iota(jnp.int32, (8, nb), 1) * LANES], axis=1).astype(BF16)

    def per_expert(e, _):
        rows = pl.ds(pl.multiple_of(e * nb, nb), nb)
        inc = _dot_nt(ones8, inc_ref[rows, :].astype(BF16))[0:1, :]
        exc = _dot_nt(ones8, exc_ref[rows, :].astype(BF16))[0:1, :]
        blk = jnp.where(slot >= exc, 1.0, 0.0) * jnp.where(slot < inc, 1.0, 0.0)
        got = _dot(blk.astype(BF16), place_ref[rows, :].astype(BF16))
        match = jnp.where(got[:, 0:LANES] * LANES + got[:, LANES:] == slot1, 1.0, 0.0)
        tok = _dot_nt(ids, jnp.concatenate([match, blk], axis=1).astype(BF16))
        idx_ref[pl.ds(e, 1), :, :] = tok[0:1, :].astype(jnp.int32).reshape(1, 1, cap)
        return 0

    lax.fori_loop(0, ne, per_expert, 0, unroll=4)


def _route(afft, cap):
    b, ne, l = afft.shape
    nb = max(l // LANES, ROUTE_BLOCKS)
    g = b * ne
    a3 = jnp.pad(afft, ((0, 0), (0, 0), (0, nb * LANES - l))).reshape(g, nb, LANES)
    u = np.arange(LANES)
    tri = jnp.asarray(u[:, None] <= u[None, :], BF16)
    r = np.arange(ne * nb)
    bd = jnp.asarray((r[:, None] // nb == r[None, :] // nb) & (r[None, :] % nb < r[:, None] % nb), BF16)
    const = lambda x: pl.BlockSpec(x.shape, lambda i: (0,) * x.ndim)
    idx = pl.pallas_call(
        functools.partial(_route_kernel, cap=cap),
        out_shape=jax.ShapeDtypeStruct((g, 1, cap), jnp.int32),
        grid=(1,),
        in_specs=[const(a3), const(tri), const(bd)],
        out_specs=pl.BlockSpec((g, 1, cap), lambda i: (0, 0, 0)),
        scratch_shapes=[pltpu.VMEM((g * nb, 2 * LANES), F32), pltpu.VMEM((g * nb, LANES), F32),
                        pltpu.VMEM((g * nb, LANES), F32)],
        compiler_params=_cparams(("arbitrary",)),
        name="route",
    )(a3, tri, bd)
    return idx.reshape(b, ne, 1, cap)


def _moe_kernel(idx_ref, aff_ref, aff_last_ref, h_ref, wg_ref, wu_ref, wd_ref, acc_ref,
                xa_ref, xb_ref, ya_ref, yb_ref, *, nb, cap, ne):
    e = pl.program_id(1)
    grp = 4

    def gather_rolled(ex, dst):
        for s in range(nb):
            def body(j, _):
                dst[pl.ds(s * cap + j, 1), :] = h_ref[s, pl.ds(idx_ref[s, ex, 0, j], 1), :]
                return 0
            lax.fori_loop(0, cap, body, 0, unroll=8)

    def gather_unrolled(ex, dst):
        for s in range(nb):
            for j in range(cap):
                dst[s * cap + j:s * cap + j + 1, :] = h_ref[s, pl.ds(idx_ref[s, ex, 0, j], 1), :]

    def scatter_group(ex, gates, src, s, j0, static):
        ts = [idx_ref[s, ex, 0, j0 + k] for k in range(grp)]
        row = (lambda k: src[s * cap + j0 + k:s * cap + j0 + k + 1, :]) if static else \
              (lambda k: src[pl.ds(s * cap + j0 + k, 1), :])
        new = [acc_ref[s, pl.ds(ts[k], 1), :] + row(k) * gates[s, 0, 0, ts[k]] for k in range(grp)]
        for k in range(grp):
            acc_ref[s, pl.ds(ts[k], 1), :] = new[k]

    def scatter_unrolled(ex, src):
        for s in range(nb):
            for j0 in range(0, cap, grp):
                scatter_group(ex, aff_ref, src, s, j0, True)

    def scatter_rolled(ex, src):
        for s in range(nb):
            def body(gi, _):
                scatter_group(ex, aff_last_ref, src, s, gi * grp, False)
                return 0
            lax.fori_loop(0, cap // grp, body, 0)

    def ffn(src, dst):
        xb = src[...].astype(BF16)
        hid = _silu(_dot(xb, wg_ref[0])) * _dot(xb, wu_ref[0])
        dst[...] = _dot(hid.astype(BF16), wd_ref[0])

    @pl.when(e == 0)
    def _():
        acc_ref[...] = jnp.zeros_like(acc_ref)
        yb_ref[...] = jnp.zeros_like(yb_ref)
        gather_rolled(0, xa_ref)

    e_next = jnp.minimum(e + 1, ne - 1)
    e_prev = jnp.maximum(e - 1, 0)

    @pl.when(e % 2 == 0)
    def _():
        gather_unrolled(e_next, xb_ref)
        ffn(xa_ref, ya_ref)
        scatter_unrolled(e_prev, yb_ref)

    @pl.when(e % 2 == 1)
    def _():
        gather_unrolled(e_next, xa_ref)
        ffn(xb_ref, yb_ref)
        scatter_unrolled(e_prev, ya_ref)

    @pl.when(e == ne - 1)
    def _():
        scatter_rolled(ne - 1, yb_ref)


def _moe(idx, afft, h2, wg, wu, wd, layer, nb):
    b, l, d = h2.shape
    ne, cap = idx.shape[1], idx.shape[3]
    ff = wg.shape[2]
    assert ne % 2 == 0 and cap % 8 == 0
    wsel = lambda bi, e: (layer * ne + e, 0, 0)
    buf = pltpu.VMEM((nb * cap, d), F32)
    aff4 = afft.reshape(b, ne, 1, l)
    return pl.pallas_call(
        functools.partial(_moe_kernel, nb=nb, cap=cap, ne=ne),
        out_shape=jax.ShapeDtypeStruct((b, l, d), F32),
        grid=(b // nb, ne),
        in_specs=[pl.BlockSpec((nb, ne, 1, cap), lambda bi, e: (bi, 0, 0, 0), memory_space=pltpu.SMEM),
                  pl.BlockSpec((nb, 1, 1, l), lambda bi, e: (bi, jnp.maximum(e - 1, 0), 0, 0),
                               memory_space=pltpu.SMEM),
                  pl.BlockSpec((nb, 1, 1, l), lambda bi, e: (bi, ne - 1, 0, 0), memory_space=pltpu.SMEM),
                  pl.BlockSpec((nb, l, d), lambda bi, e: (bi, 0, 0), pipeline_mode=pl.Buffered(1)),
                  pl.BlockSpec((1, d, ff), wsel), pl.BlockSpec((1, d, ff), wsel), pl.BlockSpec((1, ff, d), wsel)],
        out_specs=pl.BlockSpec((nb, l, d), lambda bi, e: (bi, 0, 0), pipeline_mode=pl.Buffered(1)),
        scratch_shapes=[buf, buf, buf, buf],
        compiler_params=_cparams(("arbitrary", "arbitrary")),
        name="expert_mixer",
    )(idx, aff4, aff4, h2, wg, wu, wd)


def _final_kernel(x_ref, acc_ref, g5_ref, fg_ref, o_ref):
    o_ref[0] = _rms(x_ref[0] + g5_ref[0] * acc_ref[0], fg_ref[...])


def _final(x, acc, mod, layer, fg, tm):
    b, l, d = x.shape
    tok = pl.BlockSpec((1, tm, d), lambda bi, i: (bi, i, 0))
    return pl.pallas_call(
        _final_kernel,
        out_shape=jax.ShapeDtypeStruct((b, l, d), F32),
        grid=(b, l // tm),
        in_specs=[tok, tok, pl.BlockSpec((1, 1, d), lambda bi, i: ((layer * MOD_ROWS + bi) * 6 + 5, 0, 0)),
                  pl.BlockSpec(fg.shape, lambda bi, i: (0, 0))],
        out_specs=tok,
        compiler_params=_cparams(("arbitrary", "arbitrary")),
        name="final_norm",
    )(x, acc, mod, fg)


def _partner(dh):
    q = dh // 4
    return np.concatenate([np.arange(q, 2 * q), np.arange(0, q), np.arange(3 * q, 4 * q), np.arange(2 * q, 3 * q)])


def _rope_tables(n_tok, dh, reps):
    rows = n_tok // GRID_W
    row = jnp.repeat(jnp.arange(rows, dtype=jnp.int32), GRID_W)
    col = jnp.tile(jnp.arange(GRID_W, dtype=jnp.int32), rows)
    half = dh // 2
    inv = ROPE_BASE ** (-jnp.arange(0, half, 2, dtype=jnp.float32) / half)
    ar = row.astype(jnp.float32)[:, None] * inv[None, :]
    ac = col.astype(jnp.float32)[:, None] * inv[None, :]
    cos = jnp.concatenate([jnp.cos(ar), jnp.cos(ar), jnp.cos(ac), jnp.cos(ac)], axis=1)
    sin = jnp.concatenate([-jnp.sin(ar), jnp.sin(ar), -jnp.sin(ac), jnp.sin(ac)], axis=1)
    return jnp.tile(cos, (1, reps)), jnp.tile(sin, (1, reps))


def _extend_w_in(w):
    d = w.shape[0]
    wq, wk, wv = w[:, 1440:1696], w[:, 1696:1824], w[:, 1824:1952]
    kr = w[:, 1408:1440]
    pq = np.concatenate([h * WIN_DH + _partner(WIN_DH) for h in range(WIN_Q_HEADS)])
    pk = np.concatenate([h * WIN_DH + _partner(WIN_DH) for h in range(WIN_KV_HEADS)])
    ext = jnp.concatenate([w[:, 0:1408], wq, wq[:, pq], wk, wk[:, pk], wv, kr, kr[:, _partner(MLA_ROPE)],
                           jnp.zeros((d, N_EXT - C_KR - 2 * MLA_ROPE), w.dtype)], axis=1)
    return ext.astype(BF16)


def kernel(x, c, ctx, c_ctx, norm1_g, norm2_g, ada_w, ada_b, w_in, ret_decay_f, ret_decay_b,
           mla_qnorm_g, mla_kvnorm_g, mla_w_uq, mla_w_uk, mla_w_uv, win_sink, w_out,
           router_w, exp_w_gate, exp_w_up, exp_w_down, final_g):
    b, l, d = x.shape
    lc = ctx.shape[1]
    depth = w_in.shape[0]
    assert b + 1 <= MOD_ROWS and l % 512 == 0 and lc % RET_CHUNK == 0 and l >= 3 * WINDOW

    c_all = jnp.concatenate([c, c_ctx[None, :], jnp.zeros((MOD_ROWS - b - 1, d), F32)], axis=0)
    mod = _modulation(c_all, ada_w, ada_b).reshape(depth * MOD_ROWS * 6, 1, d)
    row_x = lambda bi: bi
    row_c = lambda bi: b

    wabs = _absorbed_q_weights(mla_w_uq, mla_w_uk)
    cw, sw = _rope_tables(l, WIN_DH, LANES // WIN_DH)
    cm, sm = _rope_tables(l, MLA_ROPE, 1)
    cq, sq = _rope_tables(l, MLA_ROPE, LANES // MLA_ROPE)
    pr = np.concatenate([h * MLA_ROPE + _partner(MLA_ROPE) for h in range(MLA_HEADS)])

    tm = 1024 if l % 1024 == 0 else 512
    acc_x = acc_c = None
    for layer in range(depth):
        need_ctx = layer < depth - 1
        w_ext = _extend_w_in(w_in[layer])
        g1 = norm1_g[layer].reshape(1, d)
        qng = mla_qnorm_g[layer].reshape(1, -1)
        kvg = mla_kvnorm_g[layer].reshape(1, -1)
        uq_rope = mla_w_uq[layer][:, :, MLA_NOPE:].reshape(MLA_Q_RANK, MLA_HEADS * MLA_ROPE)
        wq_all = jnp.concatenate([wabs[layer], uq_rope, uq_rope[:, pr]], axis=1).astype(BF16)
        wuv = jnp.transpose(mla_w_uv[layer], (1, 0, 2)).astype(BF16)
        dec = jnp.broadcast_to(jnp.concatenate([ret_decay_f[layer], ret_decay_b[layer]])[:, None],
                               (2 * RET_HEADS, LANES)).astype(F32)
        wo = w_out[layer].astype(BF16)
        n2 = norm2_g[layer].reshape(1, d)
        rw = jnp.pad(router_w[layer], ((0, 0), (0, LANES - N_EXPERTS)))
        rwh = rw.astype(BF16)
        rw2 = jnp.concatenate([rwh, (rw - rwh.astype(F32)).astype(BF16)], axis=1)
        wg, wu, wd = (w.astype(BF16).reshape((-1,) + w.shape[2:]) for w in (exp_w_gate, exp_w_up, exp_w_down))

        res_x = None if layer == 0 else (acc_x, layer - 1)
        res_c = None if layer == 0 else (acc_c, layer - 1)
        x, ret_x, cqn_x, kcat_x, win_x = _inproj(x, mod, row_x, layer, g1, w_ext, qng, kvg,
                                                 (cw, sw, cm, sm), res_x, tm)
        ctx, ret_c, cqn_c, kcat_c, win_c = _inproj(ctx, mod, row_c, layer, g1, w_ext, qng, kvg,
                                                   None, res_c, min(tm, lc))
        ycf, ycb, yxf, yxb = _retention(ret_c, ret_x, dec, 8 if l % (8 * RET_CHUNK) == 0 else 1)
        mla_x, wn_x = _attention(cqn_x, wq_all, (cq, sq), jnp.concatenate([kcat_c, kcat_x], axis=1), wuv,
                                 win_x, win_x, win_c, win_sink[layer])
        x, h2x, afft_x = _outproj(x, yxf, yxb, ret_x, mla_x, wn_x, mod, row_x, layer, n2, wo, rw2, tm)
        idx_x = _route(afft_x, CAPACITY_FACTOR * l // N_EXPERTS)
        acc_x = _moe(idx_x, afft_x, h2x, wg, wu, wd, layer, 1)
        if need_ctx:
            mla_c, wn_c = _attention(cqn_c, wq_all, None, kcat_c, wuv, win_c, None, win_c, win_sink[layer])
            ctx, h2c, afft_c = _outproj(ctx, ycf, ycb, ret_c, mla_c, wn_c, mod, row_c, layer, n2, wo,
                                        rw2, min(tm, lc))
            idx_c = _route(afft_c, CAPACITY_FACTOR * lc // N_EXPERTS)
            acc_c = _moe(idx_c, afft_c, h2c, wg, wu, wd, layer, b)
    return _final(x, acc_x, mod, depth - 1, final_g.reshape(1, d), tm)
```
